```python
import math
import jax, jax.numpy as jnp
from jax import lax
import numpy as np

D_MODEL = 2048
BATCH = 8
SEQ = 2048
DEPTH = 1
DEC_BATCH = 32
DEC_SEQ = 1
PAST_LEN = 8192
PAGE_SIZE = 128

FOX_HEADS = 8
FOX_HD = 128
FOX_W = FOX_HEADS * FOX_HD
Q_BLOCK = 128
FOX_F_BIAS = 8.0
GDN_HEADS = 8
GDN_DK = 128
GDN_DV = 128
GDN_KW = GDN_HEADS * GDN_DK
GDN_VW = GDN_HEADS * GDN_DV
CONV_W = 4
CONV_CH = 2 * GDN_KW + GDN_VW
GDN_CHUNK = 64
PEER_HEADS = 8
N_KEYS = 128
N_EXPERTS = N_KEYS * N_KEYS
PEER_DK = 128
PEER_TOPK = 16
PEER_TOKEN_BLOCK = 128
PLE_DIM = 256
EPS = 1e-6

OFF_FQ = 0
OFF_FK = OFF_FQ + FOX_W
OFF_FV = OFF_FK + FOX_W
OFF_FF = OFF_FV + FOX_W
OFF_GQKV = OFF_FF + FOX_HEADS
OFF_GA = OFF_GQKV + CONV_CH
OFF_GB = OFF_GA + GDN_HEADS
OFF_GZ = OFF_GB + GDN_HEADS
OFF_GATE = OFF_GZ + GDN_VW
N_IN = OFF_GATE + 2 * D_MODEL

kernel_name = 'fox_gdn_peer_hybrid_step'


def rmsnorm(x, g):
    xf = x.astype(jnp.float32)
    y = xf * lax.rsqrt(jnp.mean(xf * xf, axis=-1, keepdims=True) + EPS)
    return (y * g.astype(jnp.float32)).astype(x.dtype)


def l2norm(x):
    xf = x.astype(jnp.float32)
    return xf * lax.rsqrt(jnp.sum(xf * xf, axis=-1, keepdims=True) + EPS)


def suffix_logf(logf, axis):
    lf = logf.astype(jnp.float32)
    return lax.cumsum(lf, axis=axis, reverse=True) - lf


def fox_inputs(proj, lw):
    B, L, _ = proj.shape
    q = proj[..., OFF_FQ:OFF_FK].reshape(B, L, FOX_HEADS, FOX_HD)
    k = proj[..., OFF_FK:OFF_FV].reshape(B, L, FOX_HEADS, FOX_HD)
    v = proj[..., OFF_FV:OFF_FF].reshape(B, L, FOX_HEADS, FOX_HD)
    logf = jax.nn.log_sigmoid(proj[..., OFF_FF:OFF_GQKV].astype(jnp.float32) + lw['fox_f_bias'].astype(jnp.float32))
    return rmsnorm(q, lw['fox_q_norm']), rmsnorm(k, lw['fox_k_norm']), v, logf


def fox_prompt(q, k, v, logf):
    B, L = q.shape[:2]
    nb = L // Q_BLOCK
    scale = FOX_HD ** -0.5
    d = suffix_logf(logf, 1).transpose(0, 2, 1)
    q_blocks = q.reshape(B, nb, Q_BLOCK, FOX_HEADS, FOX_HD).transpose(1, 0, 2, 3, 4)
    key_pos = jnp.arange(L)

    def one_block(args):
        qi, b = args
        q_pos = b * Q_BLOCK + jnp.arange(Q_BLOCK)
        dq = lax.dynamic_slice_in_dim(d, b * Q_BLOCK, Q_BLOCK, axis=2)
        s = jnp.einsum('bqhd,bkhd->bhqk', qi, k).astype(jnp.float32) * scale
        s = s + d[:, :, None, :] - dq[:, :, :, None]
        s = jnp.where(key_pos[None, :] <= q_pos[:, None], s, -jnp.inf)
        pr = jax.nn.softmax(s, axis=-1).astype(v.dtype)
        return jnp.einsum('bhqk,bkhd->bqhd', pr, v)

    o = lax.map(one_block, (q_blocks, jnp.arange(nb)))
    return o.transpose(1, 0, 2, 3, 4).reshape(B, L, FOX_W)


def fox_sample(q, k, v, logf, cache_k, cache_v, cache_logf, page_table, layer):
    n_ctx = page_table.shape[1] * PAGE_SIZE
    ds = q.shape[1]
    scale = FOX_HD ** -0.5
    key_pos = jnp.arange(n_ctx + ds)
    q_pos = n_ctx + jnp.arange(ds)
    mask = key_pos[None, :] <= q_pos[:, None]

    def one_seq(args):
        qi, ki, vi, lfi, pages = args
        kp = cache_k[layer, pages].reshape(n_ctx, FOX_HEADS, FOX_HD)
        vp = cache_v[layer, pages].reshape(n_ctx, FOX_HEADS, FOX_HD)
        lfp = cache_logf[layer, pages].reshape(n_ctx, FOX_HEADS)
        ka = jnp.concatenate([kp.astype(ki.dtype), ki], axis=0)
        va = jnp.concatenate([vp.astype(vi.dtype), vi], axis=0)
        d = suffix_logf(jnp.concatenate([lfp.astype(jnp.float32), lfi], axis=0), 0)
        s = jnp.einsum('qhd,khd->hqk', qi, ka).astype(jnp.float32) * scale
        s = s + d.T[:, None, :] - d[n_ctx:].T[:, :, None]
        s = jnp.where(mask[None], s, -jnp.inf)
        pr = jax.nn.softmax(s, axis=-1).astype(va.dtype)
        return jnp.einsum('hqk,khd->qhd', pr, va)

    o = lax.map(one_seq, (q, k, v, logf, page_table))
    return o.reshape(q.shape[0], ds, FOX_W)


def causal_conv(xc, conv_state, w_conv):
    L = xc.shape[1]
    xp = jnp.concatenate([conv_state.astype(xc.dtype), xc], axis=1)
    y = xp[:, 0:L] * w_conv[0]
    for j in range(1, CONV_W):
        y = y + xp[:, j:j + L] * w_conv[j]
    return jax.nn.silu(y), xp[:, L:]


def gdn_inputs(proj, conv_state, lw):
    B, L, _ = proj.shape
    y, new_conv = causal_conv(proj[..., OFF_GQKV:OFF_GA], conv_state, lw['gdn_conv_w'])
    q = l2norm(y[..., :GDN_KW].reshape(B, L, GDN_HEADS, GDN_DK)) * (GDN_DK ** -0.5)
    k = l2norm(y[..., GDN_KW:2 * GDN_KW].reshape(B, L, GDN_HEADS, GDN_DK))
    v = y[..., 2 * GDN_KW:].reshape(B, L, GDN_HEADS, GDN_DV).astype(jnp.float32)
    a = proj[..., OFF_GA:OFF_GB].astype(jnp.float32)
    b = proj[..., OFF_GB:OFF_GZ].astype(jnp.float32)
    g = -jnp.exp(lw['gdn_a_log'].astype(jnp.float32)) * jax.nn.softplus(a + lw['gdn_dt_bias'].astype(jnp.float32))
    beta = jax.nn.sigmoid(b)
    z = proj[..., OFF_GZ:OFF_GATE].reshape(B, L, GDN_HEADS, GDN_DV)
    return q, k, v, g, beta, z, new_conv


def gdn_chunked(q, k, v, g, beta, s0):
    B, L = q.shape[:2]
    C = GDN_CHUNK
    n = L // C

    def chunks(t):
        return t.reshape((B, n, C) + t.shape[2:]).swapaxes(2, 3).swapaxes(0, 1)

    qc, kc, vc, gc, bc = chunks(q), chunks(k), chunks(v), chunks(g), chunks(beta)
    gcum = jnp.cumsum(gc, axis=-1)
    incl = jnp.tril(jnp.ones((C, C), dtype=bool))
    strict = jnp.tril(jnp.ones((C, C), dtype=bool), -1)
    diff = gcum[..., :, None] - gcum[..., None, :]
    decay = jnp.where(incl, jnp.exp(jnp.where(incl, diff, 0.0)), 0.0)
    kb = kc * bc[..., None]
    a = jnp.where(strict, jnp.einsum('nbhid,nbhjd->nbhij', kb, kc) * decay, 0.0)
    eye = jnp.eye(C, dtype=a.dtype)
    t_inv = lax.linalg.triangular_solve(a + eye, jnp.broadcast_to(eye, a.shape), left_side=True, lower=True, unit_diagonal=True)
    u = jnp.einsum('nbhij,nbhjd->nbhid', t_inv, vc * bc[..., None])
    w = jnp.einsum('nbhij,nbhjd->nbhid', t_inv, kb * jnp.exp(gcum)[..., None])

    def step(S, xs):
        qi, ki, ui, wi, gi, di = xs
        attn = jnp.einsum('bhid,bhjd->bhij', qi, ki) * di
        v_new = ui - jnp.einsum('bhcd,bhde->bhce', wi, S)
        o = jnp.einsum('bhcd,bhde->bhce', qi * jnp.exp(gi)[..., None], S) + jnp.einsum('bhij,bhje->bhie', attn, v_new)
        g_last = gi[..., -1]
        S = S * jnp.exp(g_last)[..., None, None] + jnp.einsum('bhcd,bhce->bhde', ki * jnp.exp(g_last[..., None] - gi)[..., None], v_new)
        return S, o

    S, o = lax.scan(step, s0, (qc, kc, u, w, gcum, decay))
    return o.swapaxes(0, 1).swapaxes(2, 3).reshape(B, L, GDN_HEADS, GDN_DV), S


def gdn_recurrent(q, k, v, g, beta, s0):
    def step(S, xs):
        qt, kt, vt, gt, bt = xs
        S = S * jnp.exp(gt)[..., None, None]
        kv = jnp.einsum('bhd,bhde->bhe', kt, S)
        S = S + kt[..., :, None] * ((vt - kv) * bt[..., None])[..., None, :]
        return S, jnp.einsum('bhd,bhde->bhe', qt, S)

    xs = (q.swapaxes(0, 1), k.swapaxes(0, 1), v.swapaxes(0, 1), g.swapaxes(0, 1), beta.swapaxes(0, 1))
    S, o = lax.scan(step, s0, xs)
    return o.swapaxes(0, 1), S


def gdn_output(o, z, lw):
    B, L = o.shape[:2]
    o = rmsnorm(o, lw['gdn_norm_g']).astype(z.dtype) * jax.nn.silu(z)
    return o.reshape(B, L, GDN_VW)


def merge_branches(proj, o_fox, o_gdn, lw):
    ga = jax.nn.sigmoid(proj[..., OFF_GATE:OFF_GATE + D_MODEL])
    gb = jax.nn.sigmoid(proj[..., OFF_GATE + D_MODEL:N_IN])
    m = ga * (o_fox @ lw['w_up_fox']) + gb * (o_gdn @ lw['w_up_gdn'])
    return m @ lw['w_out']


def peer_route(h, w_q, sub_keys):
    T = h.shape[0]
    q = (h @ w_q).reshape(T, PEER_HEADS, 2, PEER_DK)
    s = jnp.einsum('thpd,hpkd->thpk', q, sub_keys).astype(jnp.float32)
    sv, si = lax.top_k(s, PEER_TOPK)
    cand = (sv[:, :, 0, :, None] + sv[:, :, 1, None, :]).reshape(T, PEER_HEADS, PEER_TOPK * PEER_TOPK)
    cv, ci = lax.top_k(cand, PEER_TOPK)
    ia = jnp.take_along_axis(si[:, :, 0], ci // PEER_TOPK, axis=-1)
    ib = jnp.take_along_axis(si[:, :, 1], ci % PEER_TOPK, axis=-1)
    return ia * N_KEYS + ib, jax.nn.softmax(cv, axis=-1)


def peer_ffn(h, lw):
    B, L, D = h.shape
    T = B * L
    hf = h.reshape(T, D)
    idx, gate = peer_route(hf, lw['peer_w_q'], lw['peer_sub_keys'])
    nb = -(-T // PEER_TOKEN_BLOCK)
    pad = nb * PEER_TOKEN_BLOCK - T
    hb = jnp.pad(hf, ((0, pad), (0, 0))).reshape(nb, PEER_TOKEN_BLOCK, D)
    ib = jnp.pad(idx, ((0, pad), (0, 0), (0, 0))).reshape(nb, PEER_TOKEN_BLOCK, PEER_HEADS, PEER_TOPK)
    gb = jnp.pad(gate, ((0, pad), (0, 0), (0, 0))).reshape(nb, PEER_TOKEN_BLOCK, PEER_HEADS, PEER_TOPK)
    eu, ev = lw['peer_u'], lw['peer_v']

    def one_block(args):
        hi, ii, gi = args
        act = jax.nn.gelu(jnp.einsum('td,thkd->thk', hi, eu[ii]).astype(jnp.float32), approximate=False)
        return jnp.einsum('thk,thkd->td', (act * gi).astype(hi.dtype), ev[ii])

    y = lax.map(one_block, (hb, ib, gb)).reshape(nb * PEER_TOKEN_BLOCK, D)[:T]
    return y.reshape(B, L, D)


def channel_and_ple(x, p_i, lw):
    x = x + peer_ffn(rmsnorm(x, lw['norm_ffn_g']), lw)
    gate = jax.nn.sigmoid(rmsnorm(x, lw['norm_ple_g']) @ lw['w_ple_gate'])
    return x + gate * (p_i.astype(x.dtype) @ lw['w_ple'])


def block_prompt(x, p_i, lw):
    B = x.shape[0]
    proj = rmsnorm(x, lw['norm_mix_g']) @ lw['w_in']
    q, k, v, logf = fox_inputs(proj, lw)
    o_fox = fox_prompt(q, k, v, logf)
    conv0 = jnp.zeros((B, CONV_W - 1, CONV_CH), x.dtype)
    gq, gk, gv, g, beta, z, conv_new = gdn_inputs(proj, conv0, lw)
    s0 = jnp.zeros((B, GDN_HEADS, GDN_DK, GDN_DV), jnp.float32)
    o_g, s_new = gdn_chunked(gq, gk, gv, g, beta, s0)
    x = x + merge_branches(proj, o_fox, gdn_output(o_g, z, lw), lw)
    x = channel_and_ple(x, p_i, lw)
    return x, k, v, logf, s_new, conv_new


def block_sample(x, p_i, cache_k, cache_v, cache_logf, s0, conv0, page_table, layer, lw):
    proj = rmsnorm(x, lw['norm_mix_g']) @ lw['w_in']
    q, k, v, logf = fox_inputs(proj, lw)
    o_fox = fox_sample(q, k, v, logf, cache_k, cache_v, cache_logf, page_table, layer)
    gq, gk, gv, g, beta, z, conv_new = gdn_inputs(proj, conv0, lw)
    o_g, s_new = gdn_recurrent(gq, gk, gv, g, beta, s0.astype(jnp.float32))
    x = x + merge_branches(proj, o_fox, gdn_output(o_g, z, lw), lw)
    x = channel_and_ple(x, p_i, lw)
    return x, k, v, logf, s_new, conv_new


def setup_inputs(seed: int = 0) -> dict:
    key = jax.random.key(seed)
    ks = jax.random.split(key, 32)
    f32 = jnp.float32
    n_pages = PAST_LEN // PAGE_SIZE
    n_used = DEC_BATCH * n_pages
    n_pool = n_used + max(n_used // 4, 1)

    def nrm(k, shape, s):
        return jax.random.normal(k, shape, f32) * s

    def gain(k, shape):
        return 1.0 + 0.02 * jax.random.normal(k, shape, f32)

    page_table = jax.random.permutation(ks[9], n_pool)[:n_used].reshape(DEC_BATCH, n_pages).astype(jnp.int32)
    dt = jnp.exp(jax.random.uniform(ks[17], (DEPTH, GDN_HEADS), f32, math.log(1e-3), math.log(1e-1)))
    return {
        'x_prompt': nrm(ks[0], (BATCH, SEQ, D_MODEL), 1.0),
        'x_sample': nrm(ks[1], (DEC_BATCH, DEC_SEQ, D_MODEL), 1.0),
        'p_prompt': nrm(ks[2], (DEPTH, BATCH, SEQ, PLE_DIM), 1.0),
        'p_sample': nrm(ks[3], (DEPTH, DEC_BATCH, DEC_SEQ, PLE_DIM), 1.0),
        'cache_fox_k': nrm(ks[4], (DEPTH, n_pool, PAGE_SIZE, FOX_HEADS, FOX_HD), 1.0),
        'cache_fox_v': nrm(ks[5], (DEPTH, n_pool, PAGE_SIZE, FOX_HEADS, FOX_HD), 1.0),
        'cache_fox_logf': jax.nn.log_sigmoid(FOX_F_BIAS + nrm(ks[6], (DEPTH, n_pool, PAGE_SIZE, FOX_HEADS), 1.0)),
        'state_gdn': nrm(ks[7], (DEPTH, DEC_BATCH, GDN_HEADS, GDN_DK, GDN_DV), 0.3),
        'state_conv': nrm(ks[8], (DEPTH, DEC_BATCH, CONV_W - 1, CONV_CH), 1.0),
        'page_table': page_table,
        'norm_mix_g': gain(ks[10], (DEPTH, D_MODEL)),
        'w_in': nrm(ks[11], (DEPTH, D_MODEL, N_IN), D_MODEL ** -0.5),
        'fox_f_bias': FOX_F_BIAS + nrm(ks[12], (DEPTH, FOX_HEADS), 0.1),
        'fox_q_norm': gain(ks[13], (DEPTH, FOX_HD)),
        'fox_k_norm': gain(ks[14], (DEPTH, FOX_HD)),
        'gdn_conv_w': nrm(ks[15], (DEPTH, CONV_W, CONV_CH), CONV_W ** -0.5),
        'gdn_a_log': jnp.log(jax.random.uniform(ks[16], (DEPTH, GDN_HEADS), f32, 1.0, 16.0)),
        'gdn_dt_bias': dt + jnp.log(-jnp.expm1(-dt)),
        'gdn_norm_g': gain(ks[18], (DEPTH, GDN_DV)),
        'w_up_fox': nrm(ks[19], (DEPTH, FOX_W, D_MODEL), FOX_W ** -0.5),
        'w_up_gdn': nrm(ks[20], (DEPTH, GDN_VW, D_MODEL), GDN_VW ** -0.5),
        'w_out': nrm(ks[21], (DEPTH, D_MODEL, D_MODEL), D_MODEL ** -0.5),
        'norm_ffn_g': gain(ks[22], (DEPTH, D_MODEL)),
        'peer_w_q': nrm(ks[23], (DEPTH, D_MODEL, PEER_HEADS * 2 * PEER_DK), D_MODEL ** -0.5),
        'peer_sub_keys': nrm(ks[24], (DEPTH, PEER_HEADS, 2, N_KEYS, PEER_DK), PEER_DK ** -0.5),
        'peer_u': nrm(ks[25], (DEPTH, N_EXPERTS, D_MODEL), D_MODEL ** -0.5),
        'peer_v': nrm(ks[26], (DEPTH, N_EXPERTS, D_MODEL), PEER_HEADS ** -0.5),
        'norm_ple_g': gain(ks[27], (DEPTH, D_MODEL)),
        'w_ple': nrm(ks[28], (DEPTH, PLE_DIM, D_MODEL), PLE_DIM ** -0.5),
        'w_ple_gate': nrm(ks[29], (DEPTH, D_MODEL, D_MODEL), D_MODEL ** -0.5),
    }


def reference(x_prompt, x_sample, p_prompt, p_sample, cache_fox_k, cache_fox_v, cache_fox_logf, state_gdn, state_conv, page_table,
              norm_mix_g, w_in, fox_f_bias, fox_q_norm, fox_k_norm, gdn_conv_w, gdn_a_log, gdn_dt_bias, gdn_norm_g,
              w_up_fox, w_up_gdn, w_out, norm_ffn_g, peer_w_q, peer_sub_keys, peer_u, peer_v, norm_ple_g, w_ple, w_ple_gate):
    xp, xs = x_prompt, x_sample
    kp_l, vp_l, lfp_l, sp_l, cp_l = [], [], [], [], []
    ks_l, vs_l, lfs_l, ss_l, cs_l = [], [], [], [], []
    for i in range(DEPTH):
        lw = {
            'norm_mix_g': norm_mix_g[i], 'w_in': w_in[i], 'fox_f_bias': fox_f_bias[i],
            'fox_q_norm': fox_q_norm[i], 'fox_k_norm': fox_k_norm[i], 'gdn_conv_w': gdn_conv_w[i],
            'gdn_a_log': gdn_a_log[i], 'gdn_dt_bias': gdn_dt_bias[i], 'gdn_norm_g': gdn_norm_g[i],
            'w_up_fox': w_up_fox[i], 'w_up_gdn': w_up_gdn[i], 'w_out': w_out[i], 'norm_ffn_g': norm_ffn_g[i],
            'peer_w_q': peer_w_q[i], 'peer_sub_keys': peer_sub_keys[i], 'peer_u': peer_u[i], 'peer_v': peer_v[i],
            'norm_ple_g': norm_ple_g[i], 'w_ple': w_ple[i], 'w_ple_gate': w_ple_gate[i],
        }
        xp, kp, vp, lfp, sp, cp = block_prompt(xp, p_prompt[i], lw)
        xs, ks_, vs_, lfs, ss, cs = block_sample(xs, p_sample[i], cache_fox_k, cache_fox_v, cache_fox_logf,
                                                 state_gdn[i], state_conv[i], page_table, i, lw)
        kp_l.append(kp); vp_l.append(vp); lfp_l.append(lfp.astype(cache_fox_logf.dtype))
        sp_l.append(sp.astype(state_gdn.dtype)); cp_l.append(cp.astype(state_conv.dtype))
        ks_l.append(ks_); vs_l.append(vs_); lfs_l.append(lfs.astype(cache_fox_logf.dtype))
        ss_l.append(ss.astype(state_gdn.dtype)); cs_l.append(cs.astype(state_conv.dtype))
    fox_k_prompt = jnp.stack(kp_l)
    fox_v_prompt = jnp.stack(vp_l)
    fox_logf_prompt = jnp.stack(lfp_l)
    gdn_state_prompt = jnp.stack(sp_l)
    conv_state_prompt = jnp.stack(cp_l)
    fox_k_sample = jnp.stack(ks_l)
    fox_v_sample = jnp.stack(vs_l)
    fox_logf_sample = jnp.stack(lfs_l)
    gdn_state_sample = jnp.stack(ss_l)
    conv_state_sample = jnp.stack(cs_l)
    return (xp, xs, fox_k_prompt, fox_v_prompt, fox_logf_prompt, gdn_state_prompt, conv_state_prompt,
            fox_k_sample, fox_v_sample, fox_logf_sample, gdn_state_sample, conv_state_sample)
```

```python
import functools
import math

import jax
import jax.numpy as jnp
from jax import lax
from jax.experimental import pallas as pl
from jax.experimental.pallas import tpu as pltpu

F32 = jnp.float32
BF16 = jnp.bfloat16
HIGHEST = lax.Precision.HIGHEST
EPS = 1e-6
NEG_BIG = -1e30

LANES = 128
SUBLANES = 8
VMEM_LIMIT = 56 * 1024 * 1024

FOX_HEADS = 8
FOX_HD = 128
FOX_W = FOX_HEADS * FOX_HD
GDN_HEADS = 8
GDN_DK = 128
GDN_DV = 128
GDN_KW = GDN_HEADS * GDN_DK
GDN_VW = GDN_HEADS * GDN_DV
CONV_W = 4
CONV_CH = 2 * GDN_KW + GDN_VW
GDN_CHUNK = 64
PEER_HEADS = 8
N_KEYS = 128
PEER_DK = 128
PEER_TOPK = 16
PEER_SEL = PEER_HEADS * PEER_TOPK

OFF_FQ = 0
OFF_FK = OFF_FQ + FOX_W
OFF_FV = OFF_FK + FOX_W
OFF_FF = OFF_FV + FOX_W
OFF_GQKV = OFF_FF + FOX_HEADS
OFF_GA = OFF_GQKV + CONV_CH
OFF_GB = OFF_GA + GDN_HEADS
OFF_GZ = OFF_GB + GDN_HEADS
OFF_GATE = OFF_GZ + GDN_VW

P_Q = 0
P_K = P_Q + FOX_W
P_V = P_K + FOX_W
P_GQKV = P_V + FOX_W
P_Z = P_GQKV + CONV_CH
P_GATE = P_Z + GDN_VW
SM_LOGF = 0
SM_G = SM_LOGF + FOX_HEADS
SM_BETA = SM_G + GDN_HEADS


def _cparams(sem):
    return pltpu.CompilerParams(dimension_semantics=sem, vmem_limit_bytes=VMEM_LIMIT)


def _sigmoid(x):
    return 1.0 / (1.0 + jnp.exp(-x))


def _silu(x):
    return x * _sigmoid(x)


def _dot(a, b, precision=None):
    return jnp.dot(a, b, preferred_element_type=F32, precision=precision)


def _dot_nt(a, b, precision=None):
    return lax.dot_general(a, b, (((1,), (1,)), ((), ())), preferred_element_type=F32, precision=precision)


def _dot_tn(a, b, precision=None):
    return lax.dot_general(a, b, (((0,), (0,)), ((), ())), preferred_element_type=F32, precision=precision)


def _normed_linear_kernel(*refs, n_extra, emit_h, epilogue):
    x_ref, g_ref, w_ref = refs[:3]
    extra = refs[3:3 + n_extra]
    out_ref = refs[3 + n_extra]
    h_out = refs[4 + n_extra] if emit_h else None
    h_sc = refs[-1]
    j = pl.program_id(1)

    @pl.when(j == 0)
    def _():
        x = x_ref[...]
        y = x * lax.rsqrt(jnp.mean(x * x, axis=-1, keepdims=True) + EPS) * g_ref[...]
        h_sc[...] = y.astype(BF16)
        if emit_h:
            h_out[...] = y.astype(h_out.dtype)

    acc = _dot(h_sc[...], w_ref[...])
    epilogue(acc, j, extra, out_ref)


def normed_linear(x, g, w, *, tm, tn, epilogue, extra=(), extra_specs=(), out_dtype=F32, emit_h=False):
    T, D = x.shape
    N = w.shape[1]
    tm = min(tm, T)
    tn = min(tn, N)
    assert T % tm == 0 and N % tn == 0
    out_shape = [jax.ShapeDtypeStruct((T, N), out_dtype)]
    out_specs = [pl.BlockSpec((tm, tn), lambda i, j: (i, j))]
    if emit_h:
        out_shape.append(jax.ShapeDtypeStruct((T, D), F32))
        out_specs.append(pl.BlockSpec((tm, D), lambda i, j: (i, 0)))
    res = pl.pallas_call(
        functools.partial(_normed_linear_kernel, n_extra=len(extra), emit_h=emit_h, epilogue=epilogue),
        grid=(T // tm, N // tn),
        in_specs=[pl.BlockSpec((tm, D), lambda i, j: (i, 0)),
                  pl.BlockSpec((1, D), lambda i, j: (0, 0)),
                  pl.BlockSpec((D, tn), lambda i, j: (0, j)),
                  *extra_specs],
        out_specs=out_specs,
        out_shape=out_shape,
        scratch_shapes=[pltpu.VMEM((tm, D), BF16)],
        compiler_params=_cparams(("parallel", "arbitrary")),
    )(x, g, w, *extra)
    return res if emit_h else res[0]


def _plain_epilogue(acc, j, extra, out_ref):
    out_ref[...] = acc.astype(out_ref.dtype)


def _proj_epilogue(acc, j, extra, out_ref, *, tn, n_qk, n_plain, n_gate):
    gain_ref, par_ref = extra

    @pl.when(j < n_qk)
    def _():
        for c in range(tn // LANES):
            blk = acc[:, c * LANES:(c + 1) * LANES]
            ms = jnp.mean(blk * blk, axis=-1, keepdims=True)
            out_ref[:, c * LANES:(c + 1) * LANES] = blk * lax.rsqrt(ms + EPS) * gain_ref[:, c * LANES:(c + 1) * LANES]

    @pl.when(jnp.logical_and(j >= n_qk, j < n_qk + n_plain))
    def _():
        out_ref[...] = acc

    @pl.when(jnp.logical_and(j >= n_qk + n_plain, j < n_qk + n_plain + n_gate))
    def _():
        out_ref[...] = _sigmoid(acc)

    @pl.when(j == n_qk + n_plain + n_gate)
    def _():
        v = acc[:, :LANES] + par_ref[0:1, :]
        lane = lax.broadcasted_iota(jnp.int32, v.shape, 1)
        tail = jnp.log1p(jnp.exp(-jnp.abs(v)))
        logf = -(jnp.maximum(-v, 0.0) + tail)
        g = -jnp.exp(par_ref[1:2, :]) * (jnp.maximum(v, 0.0) + tail)
        beta = _sigmoid(v)
        res = jnp.where(lane < SM_G, logf, jnp.where(lane < SM_BETA, g, beta))
        out_ref[:, :LANES] = res
        if tn > LANES:
            out_ref[:, LANES:] = jnp.zeros((acc.shape[0], tn - LANES), F32)


def _cumsum_kernel(sm_ref, out_ref, carry_sc, *, tr):
    @pl.when(pl.program_id(1) == 0)
    def _():
        carry_sc[...] = jnp.zeros_like(carry_sc)

    r = lax.broadcasted_iota(jnp.int32, (tr, tr), 0)
    c = lax.broadcasted_iota(jnp.int32, (tr, tr), 1)
    tri = (c <= r).astype(F32)
    cs = _dot(tri, sm_ref[...], HIGHEST) + carry_sc[...]
    out_ref[...] = cs
    carry_sc[...] = cs[tr - 1:tr, :]


def seq_cumsum(P, B, L, col_block):
    tr = min(256, L)
    nb = L // tr
    return pl.pallas_call(
        functools.partial(_cumsum_kernel, tr=tr),
        grid=(B, nb),
        in_specs=[pl.BlockSpec((tr, LANES), lambda b, i: (b * nb + i, col_block))],
        out_specs=pl.BlockSpec((tr, LANES), lambda b, i: (b * nb + i, 0)),
        out_shape=jax.ShapeDtypeStruct((B * L, LANES), F32),
        scratch_shapes=[pltpu.VMEM((1, LANES), F32)],
        compiler_params=_cparams(("parallel", "arbitrary")),
    )(P)


def _fox_prompt_kernel(q_ref, k_ref, v_ref, cq_ref, ck_ref, o_ref, m_sc, l_sc, acc_sc, *, tq, tk, scale):
    qi = pl.program_id(1)
    kj = pl.program_id(2)

    @pl.when(kj == 0)
    def _():
        m_sc[...] = jnp.full(m_sc.shape, NEG_BIG, F32)
        l_sc[...] = jnp.zeros_like(l_sc)
        acc_sc[...] = jnp.zeros_like(acc_sc)

    @pl.when(kj * tk <= qi * tq + tq - 1)
    def _():
        q_pos = qi * tq + lax.broadcasted_iota(jnp.int32, (tq, tk), 0)
        k_pos = kj * tk + lax.broadcasted_iota(jnp.int32, (tq, tk), 1)
        keep = k_pos <= q_pos
        cq = cq_ref[...]
        ck = ck_ref[0]
        for h in range(FOX_HEADS):
            sl = slice(h * FOX_HD, (h + 1) * FOX_HD)
            s = _dot_nt(q_ref[:, sl].astype(BF16), k_ref[:, sl].astype(BF16)) * scale
            s = s + (cq[:, h:h + 1] - ck[h:h + 1, :])
            s = jnp.where(keep, s, NEG_BIG)
            m_old = m_sc[h]
            m_new = jnp.maximum(m_old, jnp.max(s, axis=-1, keepdims=True))
            alpha = jnp.exp(m_old - m_new)
            p = jnp.exp(s - m_new)
            l_sc[h] = alpha * l_sc[h] + jnp.sum(p, axis=-1, keepdims=True)
            acc_sc[:, sl] = alpha * acc_sc[:, sl] + _dot(p.astype(BF16), v_ref[:, sl].astype(BF16))
            m_sc[h] = m_new

    @pl.when(kj == pl.num_programs(2) - 1)
    def _():
        for h in range(FOX_HEADS):
            sl = slice(h * FOX_HD, (h + 1) * FOX_HD)
            o_ref[:, sl] = (acc_sc[:, sl] / l_sc[h]).astype(o_ref.dtype)


def fox_prompt_attention(P, c_col, c_row, B, L):
    tq = min(256, L)
    tk = min(512, L)
    nq, nk = L // tq, L // tk

    def kv_block(b, qi, kj):
        last = (qi * tq + tq - 1) // tk
        return b * nk + jnp.minimum(kj, last)

    return pl.pallas_call(
        functools.partial(_fox_prompt_kernel, tq=tq, tk=tk, scale=FOX_HD ** -0.5),
        grid=(B, nq, nk),
        in_specs=[pl.BlockSpec((tq, FOX_W), lambda b, qi, kj: (b * nq + qi, P_Q // FOX_W)),
                  pl.BlockSpec((tk, FOX_W), lambda b, qi, kj: (kv_block(b, qi, kj), P_K // FOX_W)),
                  pl.BlockSpec((tk, FOX_W), lambda b, qi, kj: (kv_block(b, qi, kj), P_V // FOX_W)),
                  pl.BlockSpec((tq, LANES), lambda b, qi, kj: (b * nq + qi, 0)),
                  pl.BlockSpec((1, FOX_HEADS, tk), lambda b, qi, kj: (b, 0, jnp.minimum(kj, (qi * tq + tq - 1) // tk)))],
        out_specs=pl.BlockSpec((tq, FOX_W), lambda b, qi, kj: (b * nq + qi, 0)),
        out_shape=jax.ShapeDtypeStruct((B * L, FOX_W), BF16),
        scratch_shapes=[pltpu.VMEM((FOX_HEADS, tq, 1), F32),
                        pltpu.VMEM((FOX_HEADS, tq, 1), F32),
                        pltpu.VMEM((tq, FOX_W), F32)],
        compiler_params=_cparams(("parallel", "parallel", "arbitrary")),
    )(P, P, P, c_col, c_row)


def _gdn_prep_kernel(x_ref, w_ref, o_ref):
    c = pl.program_id(1)
    x = x_ref[...]
    w = w_ref[...]
    row = lax.broadcasted_iota(jnp.int32, x.shape, 0)
    y = x * w[CONV_W - 1:CONV_W, :]
    for s in range(1, CONV_W):
        shifted = jnp.where(row >= s, pltpu.roll(x, s, 0), 0.0)
        y = y + shifted * w[CONV_W - 1 - s:CONV_W - s, :]
    y = _silu(y)
    normed = y * lax.rsqrt(jnp.sum(y * y, axis=-1, keepdims=True) + EPS)
    q_scale = jnp.where(c < GDN_HEADS, GDN_DK ** -0.5, 1.0)
    o_ref[...] = jnp.where(c < 2 * GDN_HEADS, normed * q_scale, y)


def gdn_prep(P, conv_w, B, L):
    nch = CONV_CH // LANES
    return pl.pallas_call(
        _gdn_prep_kernel,
        grid=(B, nch),
        in_specs=[pl.BlockSpec((L, LANES), lambda b, c: (b, P_GQKV // LANES + c)),
                  pl.BlockSpec((CONV_W, LANES), lambda b, c: (0, c))],
        out_specs=pl.BlockSpec((L, LANES), lambda b, c: (b, c)),
        out_shape=jax.ShapeDtypeStruct((B * L, CONV_CH), F32),
        compiler_params=_cparams(("parallel", "parallel")),
    )(P, conv_w)


def _gdn_out(o, z, gn):
    y = o * lax.rsqrt(jnp.mean(o * o, axis=-1, keepdims=True) + EPS) * gn
    return y * _silu(z)


def _gdn_chunk_kernel(y_ref, sm_ref, z_ref, gn_ref, o_ref, s_out_ref, s_sc):
    ci = pl.program_id(1)
    C = GDN_CHUNK

    @pl.when(ci == 0)
    def _():
        s_sc[...] = jnp.zeros_like(s_sc)

    r = lax.broadcasted_iota(jnp.int32, (C, C), 0)
    c = lax.broadcasted_iota(jnp.int32, (C, C), 1)
    incl = c <= r
    strict = c < r
    eye = (c == r).astype(F32)
    sm = sm_ref[...]
    gcum_all = _dot(incl.astype(F32), sm, HIGHEST)
    lane = lax.broadcasted_iota(jnp.int32, (C, LANES), 1)
    gn = gn_ref[...]
    for h in range(GDN_HEADS):
        q = y_ref[:, h * GDN_DK:(h + 1) * GDN_DK]
        k = y_ref[:, GDN_KW + h * GDN_DK:GDN_KW + (h + 1) * GDN_DK]
        v = y_ref[:, 2 * GDN_KW + h * GDN_DV:2 * GDN_KW + (h + 1) * GDN_DV]
        gc = gcum_all[:, SM_G + h:SM_G + h + 1]
        gr = _dot_nt((lane == SM_G + h).astype(F32), gcum_all, HIGHEST)
        beta = sm[:, SM_BETA + h:SM_BETA + h + 1]
        g_last = gc[C - 1:C, :]
        decay = jnp.where(incl, jnp.exp(jnp.where(incl, gc - gr, 0.0)), 0.0)
        kb = k * beta
        a = jnp.where(strict, _dot_nt(kb, k, HIGHEST) * decay, 0.0)
        pw = -a
        t_inv = eye + pw
        for _ in range(int(math.log2(C)) - 1):
            pw = _dot(pw, pw, HIGHEST)
            t_inv = t_inv + _dot(t_inv, pw, HIGHEST)
        u = _dot(t_inv, v * beta, HIGHEST)
        w = _dot(t_inv, kb * jnp.exp(gc), HIGHEST)
        attn = _dot_nt(q, k, HIGHEST) * decay
        S = s_sc[h]
        v_new = u - _dot(w, S, HIGHEST)
        o = _dot(q * jnp.exp(gc), S, HIGHEST) + _dot(attn, v_new, HIGHEST)
        kd = k * jnp.exp(g_last - gc)
        s_sc[h] = S * jnp.exp(g_last) + _dot_tn(kd, v_new, HIGHEST)
        o_ref[:, h * GDN_DV:(h + 1) * GDN_DV] = _gdn_out(o, z_ref[:, h * GDN_DV:(h + 1) * GDN_DV], gn).astype(o_ref.dtype)

    @pl.when(ci == pl.num_programs(1) - 1)
    def _():
        s_out_ref[0] = s_sc[...]


def gdn_chunked(Y, P, gn, B, L, sm_block):
    C = GDN_CHUNK
    n = L // C
    return pl.pallas_call(
        _gdn_chunk_kernel,
        grid=(B, n),
        in_specs=[pl.BlockSpec((C, CONV_CH), lambda b, i: (b * n + i, 0)),
                  pl.BlockSpec((C, LANES), lambda b, i: (b * n + i, sm_block)),
                  pl.BlockSpec((C, GDN_VW), lambda b, i: (b * n + i, P_Z // GDN_VW)),
                  pl.BlockSpec((1, GDN_DV), lambda b, i: (0, 0))],
        out_specs=[pl.BlockSpec((C, GDN_VW), lambda b, i: (b * n + i, 0)),
                   pl.BlockSpec((1, GDN_HEADS, GDN_DK, GDN_DV), lambda b, i: (b, 0, 0, 0))],
        out_shape=[jax.ShapeDtypeStruct((B * L, GDN_VW), BF16),
                   jax.ShapeDtypeStruct((B, GDN_HEADS, GDN_DK, GDN_DV), F32)],
        scratch_shapes=[pltpu.VMEM((GDN_HEADS, GDN_DK, GDN_DV), F32)],
        compiler_params=_cparams(("parallel", "arbitrary")),
    )(Y, P, P, gn)


def _merge_up_kernel(of_ref, og_ref, wf_ref, wg_ref, ga_ref, gb_ref, m_ref):
    m = ga_ref[...] * _dot(of_ref[...], wf_ref[...]) + gb_ref[...] * _dot(og_ref[...], wg_ref[...])
    m_ref[...] = m.astype(m_ref.dtype)


def merge_up(o_fox, o_gdn, w_up_fox, w_up_gdn, P, D, tm, tn):
    T = o_fox.shape[0]
    tm = min(tm, T)
    gate_block = P_GATE // tn
    return pl.pallas_call(
        _merge_up_kernel,
        grid=(T // tm, D // tn),
        in_specs=[pl.BlockSpec((tm, FOX_W), lambda i, j: (i, 0)),
                  pl.BlockSpec((tm, GDN_VW), lambda i, j: (i, 0)),
                  pl.BlockSpec((FOX_W, tn), lambda i, j: (0, j)),
                  pl.BlockSpec((GDN_VW, tn), lambda i, j: (0, j)),
                  pl.BlockSpec((tm, tn), lambda i, j: (i, gate_block + j)),
                  pl.BlockSpec((tm, tn), lambda i, j: (i, gate_block + D // tn + j))],
        out_specs=pl.BlockSpec((tm, tn), lambda i, j: (i, j)),
        out_shape=jax.ShapeDtypeStruct((T, D), BF16),
        compiler_params=_cparams(("parallel", "parallel")),
    )(o_fox, o_gdn, w_up_fox, w_up_gdn, P, P)


def _linear_residual_kernel(a_ref, w_ref, x_ref, o_ref):
    o_ref[...] = x_ref[...] + _dot(a_ref[...], w_ref[...])


def linear_residual(a, w, x, tm, tn):
    T, K = a.shape
    N = w.shape[1]
    tm = min(tm, T)
    return pl.pallas_call(
        _linear_residual_kernel,
        grid=(T // tm, N // tn),
        in_specs=[pl.BlockSpec((tm, K), lambda i, j: (i, 0)),
                  pl.BlockSpec((K, tn), lambda i, j: (0, j)),
                  pl.BlockSpec((tm, tn), lambda i, j: (i, j))],
        out_specs=pl.BlockSpec((tm, tn), lambda i, j: (i, j)),
        out_shape=jax.ShapeDtypeStruct((T, N), F32),
        compiler_params=_cparams(("parallel", "parallel")),
    )(a, w, x)


def _top16(s, val_sc, idx_sc):
    n = s.shape[0]
    row = lax.broadcasted_iota(jnp.int32, s.shape, 0).astype(F32)
    for it in range(PEER_TOPK):
        m = jnp.max(s, axis=0, keepdims=True)
        idx = jnp.min(jnp.where(s == m, row, float(n)), axis=0, keepdims=True)
        val_sc[it:it + 1, :] = m
        idx_sc[it:it + 1, :] = idx
        s = jnp.where(row == idx, -jnp.inf, s)


def _peer_route_kernel(q_ref, keys_ref, idx_ref, gate_ref, va_sc, ia_sc, vb_sc, ib_sc, vc_sc, ic_sc):
    K = PEER_TOPK
    sa = _dot_nt(keys_ref[0, 0], q_ref[:, :PEER_DK])
    _top16(sa, va_sc, ia_sc)
    sb = _dot_nt(keys_ref[0, 1], q_ref[:, PEER_DK:])
    _top16(sb, vb_sc, ib_sc)
    va, vb = va_sc[...], vb_sc[...]
    cand = jnp.concatenate([va[i:i + 1, :] + vb for i in range(K)], axis=0)
    _top16(cand, vc_sc, ic_sc)
    cv, ci = vc_sc[...], ic_sc[...]
    hi = jnp.floor(ci * (1.0 / K))
    lo = ci - hi * K
    ia_all, ib_all = ia_sc[...], ib_sc[...]
    ia = jnp.zeros_like(ci)
    ib = jnp.zeros_like(ci)
    for i in range(K):
        ia = jnp.where(hi == i, ia_all[i:i + 1, :], ia)
        ib = jnp.where(lo == i, ib_all[i:i + 1, :], ib)
    idx_ref[0] = (ia * N_KEYS + ib).astype(jnp.int32)
    e = jnp.exp(cv - cv[0:1, :])
    gate_ref[0] = e / jnp.sum(e, axis=0, keepdims=True)


def peer_route(q, sub_keys, tb):
    T = q.shape[0]
    tb = min(tb, T)
    K = PEER_TOPK
    return pl.pallas_call(
        _peer_route_kernel,
        grid=(T // tb, PEER_HEADS),
        in_specs=[pl.BlockSpec((tb, 2 * PEER_DK), lambda i, h: (i, h)),
                  pl.BlockSpec((1, 2, N_KEYS, PEER_DK), lambda i, h: (h, 0, 0, 0))],
        out_specs=[pl.BlockSpec((1, K, tb), lambda i, h: (h, 0, i)),
                   pl.BlockSpec((1, K, tb), lambda i, h: (h, 0, i))],
        out_shape=[jax.ShapeDtypeStruct((PEER_HEADS, K, T), jnp.int32),
                   jax.ShapeDtypeStruct((PEER_HEADS, K, T), F32)],
        scratch_shapes=[pltpu.VMEM((K, tb), F32) for _ in range(6)],
        compiler_params=_cparams(("parallel", "parallel")),
    )(q, sub_keys)


PEER_TB = 8
GATE_TILE = 128


def _peer_expert_kernel(idx_ref, idx_next_ref, h_ref, gate_ref, x_ref, uv_hbm, o_ref, buf, sem, *, D):
    i = pl.program_id(0)
    n = pl.num_programs(0)
    slot = i % 2
    rows = PEER_TB * PEER_SEL

    def row_copy(idx, slot_, r):
        t = r // PEER_SEL
        k = r % PEER_SEL
        return pltpu.make_async_copy(uv_hbm.at[pl.ds(idx[t, k], 1), :], buf.at[slot_, pl.ds(r, 1), :], sem.at[slot_])

    def start_fetch(idx, slot_):
        def body(r, carry):
            row_copy(idx, slot_, r).start()
            return carry
        lax.fori_loop(0, rows, body, 0)

    @pl.when(i == 0)
    def _():
        start_fetch(idx_ref, 0)

    @pl.when(i + 1 < n)
    def _():
        start_fetch(idx_next_ref, 1 - slot)

    pltpu.make_async_copy(uv_hbm.at[pl.ds(0, rows), :], buf.at[slot], sem.at[slot]).wait()

    tok = lax.broadcasted_iota(jnp.int32, (GATE_TILE, PEER_TB), 0)
    col = lax.broadcasted_iota(jnp.int32, (GATE_TILE, PEER_TB), 1)
    first = (i % (GATE_TILE // PEER_TB)) * PEER_TB
    onehot = (tok == first + col).astype(F32)
    gates = _dot(gate_ref[...], onehot, HIGHEST)

    for t in range(PEER_TB):
        U = buf[slot, t * PEER_SEL:(t + 1) * PEER_SEL, :D]
        V = buf[slot, t * PEER_SEL:(t + 1) * PEER_SEL, D:]
        h = h_ref[t:t + 1, :]
        s = jnp.sum(U * h, axis=-1, keepdims=True)
        act = 0.5 * s * (1.0 + lax.erf(s * (2.0 ** -0.5)))
        a = act * gates[:, t:t + 1]
        o_ref[t:t + 1, :] = x_ref[t:t + 1, :] + jnp.sum(V * a, axis=0, keepdims=True)


def peer_experts(idx, gate_t, h, x, uv):
    T, D = x.shape
    n = T // PEER_TB
    rows = PEER_TB * PEER_SEL
    per_tile = GATE_TILE // PEER_TB
    return pl.pallas_call(
        functools.partial(_peer_expert_kernel, D=D),
        grid=(n,),
        in_specs=[pl.BlockSpec((PEER_TB, PEER_SEL), lambda i: (i, 0), memory_space=pltpu.SMEM),
                  pl.BlockSpec((PEER_TB, PEER_SEL), lambda i: (jnp.minimum(i + 1, n - 1), 0), memory_space=pltpu.SMEM),
                  pl.BlockSpec((PEER_TB, D), lambda i: (i, 0)),
                  pl.BlockSpec((PEER_SEL, GATE_TILE), lambda i: (0, i // per_tile)),
                  pl.BlockSpec((PEER_TB, D), lambda i: (i, 0)),
                  pl.BlockSpec(memory_space=pl.ANY)],
        out_specs=pl.BlockSpec((PEER_TB, D), lambda i: (i, 0)),
        out_shape=jax.ShapeDtypeStruct((T, D), F32),
        scratch_shapes=[pltpu.VMEM((2, rows, 2 * D), F32), pltpu.SemaphoreType.DMA((2,))],
        compiler_params=_cparams(("arbitrary",)),
    )(idx, idx, h, gate_t, x, uv)


def _ple_epilogue(acc, j, extra, out_ref):
    x_ref, p_ref, wp_ref = extra
    out_ref[...] = x_ref[...] + _sigmoid(acc) * _dot(p_ref[...], wp_ref[...])


def _fox_sample_kernel(pt_ref, q_ref, ks_ref, vs_ref, lfs_ref, kc_ref, vc_ref, lfc_ref, o_ref,
                       m_sc, l_sc, acc_sc, carry_sc, *, scale):
    j = pl.program_id(1)
    H, PS, HD = FOX_HEADS, kc_ref.shape[0], FOX_HD
    q = q_ref[0]

    @pl.when(j == 0)
    def _():
        m_sc[...] = jnp.sum(q * ks_ref[0], axis=-1, keepdims=True) * scale
        l_sc[...] = jnp.ones_like(l_sc)
        acc_sc[...] = vs_ref[0]
        carry_sc[...] = lfs_ref[0][:, 0:1]

    ones = jnp.ones((HD, LANES), F32)
    pos3 = lax.broadcasted_iota(jnp.int32, (PS, H, LANES), 0)
    lane3 = lax.broadcasted_iota(jnp.int32, (PS, H, LANES), 2)
    diag = pos3 == lane3
    qk = _dot((kc_ref[...] * q[None]).reshape(PS * H, HD), ones, HIGHEST).reshape(PS, H, LANES)
    s_t = jnp.sum(jnp.where(diag, qk, 0.0), axis=0) * scale
    lf_t = lfc_ref[0]
    jj = lax.broadcasted_iota(jnp.int32, (PS, PS), 0)
    pp = lax.broadcasted_iota(jnp.int32, (PS, PS), 1)
    d_t = _dot(lf_t, (jj > pp).astype(F32), HIGHEST) + carry_sc[...]
    s_t = s_t + d_t
    m_old = m_sc[...]
    m_new = jnp.maximum(m_old, jnp.max(s_t, axis=-1, keepdims=True))
    alpha = jnp.exp(m_old - m_new)
    p_t = jnp.exp(s_t - m_new)
    l_sc[...] = alpha * l_sc[...] + jnp.sum(p_t, axis=-1, keepdims=True)
    p3 = _dot(jnp.where(diag, p_t[None], 0.0).reshape(PS * H, LANES), ones, HIGHEST).reshape(PS, H, LANES)
    acc_sc[...] = alpha * acc_sc[...] + jnp.sum(p3 * vc_ref[...], axis=0)
    m_sc[...] = m_new
    carry_sc[...] = carry_sc[...] + jnp.sum(lf_t, axis=-1, keepdims=True)

    @pl.when(j == pl.num_programs(1) - 1)
    def _():
        o_ref[0] = (acc_sc[...] / l_sc[...]).astype(o_ref.dtype)


def fox_sample_attention(page_table, q, k_self, v_self, lf_self, cache_k, cache_v, cache_lf_t, layer):
    Bs, n_pages = page_table.shape
    PS = cache_k.shape[2]
    H, HD = FOX_HEADS, FOX_HD

    def page(b, j, pt):
        return pt[b, n_pages - 1 - j]

    grid_spec = pltpu.PrefetchScalarGridSpec(
        num_scalar_prefetch=1,
        grid=(Bs, n_pages),
        in_specs=[pl.BlockSpec((1, H, HD), lambda b, j, pt: (b, 0, 0)),
                  pl.BlockSpec((1, H, HD), lambda b, j, pt: (b, 0, 0)),
                  pl.BlockSpec((1, H, HD), lambda b, j, pt: (b, 0, 0)),
                  pl.BlockSpec((1, H, LANES), lambda b, j, pt: (b, 0, 0)),
                  pl.BlockSpec((None, None, PS, H, HD), lambda b, j, pt: (layer, page(b, j, pt), 0, 0, 0)),
                  pl.BlockSpec((None, None, PS, H, HD), lambda b, j, pt: (layer, page(b, j, pt), 0, 0, 0)),
                  pl.BlockSpec((1, H, PS), lambda b, j, pt: (page(b, j, pt), 0, 0))],
        out_specs=pl.BlockSpec((1, H, HD), lambda b, j, pt: (b, 0, 0)),
        scratch_shapes=[pltpu.VMEM((H, 1), F32), pltpu.VMEM((H, 1), F32),
                        pltpu.VMEM((H, HD), F32), pltpu.VMEM((H, 1), F32)],
    )
    return pl.pallas_call(
        functools.partial(_fox_sample_kernel, scale=FOX_HD ** -0.5),
        grid_spec=grid_spec,
        out_shape=jax.ShapeDtypeStruct((Bs, H, HD), BF16),
        compiler_params=_cparams(("parallel", "arbitrary")),
    )(page_table, q, k_self, v_self, lf_self, cache_k, cache_v, cache_lf_t)


def _gdn_sample_kernel(x_ref, cs_ref, w_ref, sm_ref, z_ref, gn_ref, s_ref, o_ref, s_out_ref, conv_out_ref):
    x = x_ref[0]
    cs = cs_ref[0]
    w = w_ref[...]
    y = x * w[CONV_W - 1]
    for jx in range(CONV_W - 1):
        y = y + cs[jx] * w[jx]
    y = _silu(y)
    conv_out_ref[0, 0:CONV_W - 2] = cs[1:]
    conv_out_ref[0, CONV_W - 2] = x
    nh = GDN_HEADS
    normed = y * lax.rsqrt(jnp.sum(y * y, axis=-1, keepdims=True) + EPS)
    rowid = lax.broadcasted_iota(jnp.int32, normed.shape, 0)
    qk = jnp.where(rowid < nh, normed * (GDN_DK ** -0.5), normed)
    pad = jnp.zeros((LANES - 2 * nh, GDN_DK), F32)
    qk_t = jnp.concatenate([qk[:2 * nh], pad], axis=0).T
    sm = sm_ref[0]
    outs = []
    for h in range(nh):
        q_col = qk_t[:, h:h + 1]
        k_col = qk_t[:, nh + h:nh + h + 1]
        v_row = y[2 * nh + h:2 * nh + h + 1, :]
        g = sm[:, SM_G + h:SM_G + h + 1]
        beta = sm[:, SM_BETA + h:SM_BETA + h + 1]
        S = s_ref[0, h] * jnp.exp(g)
        kv = jnp.sum(k_col * S, axis=0, keepdims=True)
        S = S + k_col * ((v_row - kv) * beta)
        s_out_ref[0, h] = S
        outs.append(jnp.sum(q_col * S, axis=0, keepdims=True))
    o = jnp.concatenate(outs, axis=0)
    o_ref[0] = _gdn_out(o, z_ref[0], gn_ref[...]).astype(o_ref.dtype)


def gdn_sample(x24, conv_state, conv_w, sm, z, gn, state):
    Bs = x24.shape[0]
    nch = CONV_CH // LANES
    H = GDN_HEADS
    return pl.pallas_call(
        _gdn_sample_kernel,
        grid=(Bs,),
        in_specs=[pl.BlockSpec((1, nch, LANES), lambda b: (b, 0, 0)),
                  pl.BlockSpec((1, CONV_W - 1, nch, LANES), lambda b: (b, 0, 0, 0)),
                  pl.BlockSpec((CONV_W, nch, LANES), lambda b: (0, 0, 0)),
                  pl.BlockSpec((1, 1, LANES), lambda b: (b, 0, 0)),
                  pl.BlockSpec((1, H, GDN_DV), lambda b: (b, 0, 0)),
                  pl.BlockSpec((1, GDN_DV), lambda b: (0, 0)),
                  pl.BlockSpec((1, H, GDN_DK, GDN_DV), lambda b: (b, 0, 0, 0))],
        out_specs=[pl.BlockSpec((1, H, GDN_DV), lambda b: (b, 0, 0)),
                   pl.BlockSpec((1, H, GDN_DK, GDN_DV), lambda b: (b, 0, 0, 0)),
                   pl.BlockSpec((1, CONV_W - 1, nch, LANES), lambda b: (b, 0, 0, 0))],
        out_shape=[jax.ShapeDtypeStruct((Bs, H, GDN_DV), BF16),
                   jax.ShapeDtypeStruct((Bs, H, GDN_DK, GDN_DV), F32),
                   jax.ShapeDtypeStruct((Bs, CONV_W - 1, nch, LANES), F32)],
        compiler_params=_cparams(("parallel",)),
    )(x24, conv_state, conv_w, sm, z, gn, state)


PROJ_TN = 512


def _pack_layer(lw, D):
    w_in = lw['w_in']
    small = jnp.concatenate([w_in[:, OFF_FF:OFF_GQKV], w_in[:, OFF_GA:OFF_GB], w_in[:, OFF_GB:OFF_GZ]], axis=1)
    small = jnp.pad(small, ((0, 0), (0, PROJ_TN - small.shape[1])))
    w_proj = jnp.concatenate([w_in[:, OFF_FQ:OFF_FF], w_in[:, OFF_GQKV:OFF_GA], w_in[:, OFF_GZ:], small], axis=1).astype(BF16)
    n_main = P_GATE + 2 * D
    gains = jnp.concatenate([jnp.tile(lw['fox_q_norm'], FOX_HEADS), jnp.tile(lw['fox_k_norm'], FOX_HEADS),
                             jnp.zeros((n_main + PROJ_TN - 2 * FOX_W,), F32)])[None, :]
    zeros = jnp.zeros((LANES - SM_BETA,), F32)
    par = jnp.stack([jnp.concatenate([lw['fox_f_bias'], lw['gdn_dt_bias'], zeros]),
                     jnp.concatenate([jnp.zeros((SM_G,), F32), lw['gdn_a_log'], zeros])])
    par = jnp.pad(par, ((0, SUBLANES - 2), (0, 0)))
    return dict(
        w_proj=w_proj, gains=gains, par=par, n_main=n_main,
        norm_mix_g=lw['norm_mix_g'][None, :], conv_w=lw['gdn_conv_w'], gdn_norm_g=lw['gdn_norm_g'][None, :],
        w_up_fox=lw['w_up_fox'].astype(BF16), w_up_gdn=lw['w_up_gdn'].astype(BF16), w_out=lw['w_out'].astype(BF16),
        norm_ffn_g=lw['norm_ffn_g'][None, :], peer_w_q=lw['peer_w_q'].astype(BF16),
        sub_keys=lw['peer_sub_keys'].astype(BF16),
        uv=jnp.concatenate([lw['peer_u'], lw['peer_v']], axis=1),
        norm_ple_g=lw['norm_ple_g'][None, :], w_ple=lw['w_ple'].astype(BF16), w_ple_gate=lw['w_ple_gate'].astype(BF16),
    )


def _project(x, pk, tm):
    D = x.shape[1]
    tn = PROJ_TN
    n_qk = 2 * FOX_W // tn
    n_gate = 2 * D // tn
    n_plain = pk['n_main'] // tn - n_qk - n_gate
    epi = functools.partial(_proj_epilogue, tn=tn, n_qk=n_qk, n_plain=n_plain, n_gate=n_gate)
    return normed_linear(
        x, pk['norm_mix_g'], pk['w_proj'], tm=tm, tn=tn, epilogue=epi,
        extra=(pk['gains'], pk['par']),
        extra_specs=(pl.BlockSpec((1, tn), lambda i, j: (0, j)), pl.BlockSpec((SUBLANES, LANES), lambda i, j: (0, 0))))


def _channel_and_ple(x1, p, pk, tm):
    T, D = x1.shape
    Tp = -(-T // GATE_TILE) * GATE_TILE
    q, hn = normed_linear(x1, pk['norm_ffn_g'], pk['peer_w_q'], tm=tm, tn=512, epilogue=_plain_epilogue, out_dtype=BF16, emit_h=True)
    if Tp != T:
        q = jnp.pad(q, ((0, Tp - T), (0, 0)))
    idx_t, gate_t = peer_route(q, pk['sub_keys'], tb=512)
    idx = idx_t.reshape(PEER_SEL, Tp).T[:T]
    x2 = peer_experts(idx, gate_t.reshape(PEER_SEL, Tp), hn, x1, pk['uv'])
    tn = 512
    return normed_linear(
        x2, pk['norm_ple_g'], pk['w_ple_gate'], tm=tm, tn=tn, epilogue=_ple_epilogue,
        extra=(x2, p.astype(BF16), pk['w_ple']),
        extra_specs=(pl.BlockSpec((min(tm, T), tn), lambda i, j: (i, j)),
                     pl.BlockSpec((min(tm, T), p.shape[1]), lambda i, j: (i, 0)),
                     pl.BlockSpec((p.shape[1], tn), lambda i, j: (0, j))))


def _merge(x, o_fox, o_gdn, P, pk, tm):
    D = x.shape[1]
    m = merge_up(o_fox, o_gdn, pk['w_up_fox'], pk['w_up_gdn'], P, D, tm, 512)
    return linear_residual(m, pk['w_out'], x, tm, 512)


def _block_prompt(x3, p3, pk):
    B, L, D = x3.shape
    T = B * L
    x = x3.reshape(T, D)
    sm_block = pk['n_main'] // LANES
    P = _project(x, pk, tm=512)
    c_col = seq_cumsum(P, B, L, sm_block)
    c_row = c_col.reshape(B, L, LANES)[:, :, SM_LOGF:SM_LOGF + FOX_HEADS].transpose(0, 2, 1)
    o_fox = fox_prompt_attention(P, c_col, c_row, B, L)
    Y = gdn_prep(P, pk['conv_w'], B, L)
    o_gdn, s_new = gdn_chunked(Y, P, pk['gdn_norm_g'], B, L, sm_block)
    x1 = _merge(x, o_fox, o_gdn, P, pk, tm=512)
    y = _channel_and_ple(x1, p3.reshape(T, -1), pk, tm=512)
    k = P[:, P_K:P_K + FOX_W].reshape(B, L, FOX_HEADS, FOX_HD)
    v = P[:, P_V:P_V + FOX_W].reshape(B, L, FOX_HEADS, FOX_HD)
    logf = P[:, pk['n_main'] + SM_LOGF:pk['n_main'] + SM_LOGF + FOX_HEADS].reshape(B, L, FOX_HEADS)
    conv_new = P[:, P_GQKV:P_GQKV + CONV_CH].reshape(B, L, CONV_CH)[:, L - (CONV_W - 1):]
    return y.reshape(B, L, D), k, v, logf, s_new, conv_new


def _block_sample(x3, p3, cache_k, cache_v, cache_logf, s0, conv0, page_table, layer, pk):
    Bs, Ls, D = x3.shape
    assert Ls == 1
    x = x3.reshape(Bs, D)
    n_main = pk['n_main']
    nch = CONV_CH // LANES
    P = _project(x, pk, tm=Bs)
    q = P[:, P_Q:P_Q + FOX_W].reshape(Bs, FOX_HEADS, FOX_HD)
    k = P[:, P_K:P_K + FOX_W].reshape(Bs, FOX_HEADS, FOX_HD)
    v = P[:, P_V:P_V + FOX_W].reshape(Bs, FOX_HEADS, FOX_HD)
    sm = P[:, n_main:n_main + LANES]
    logf = sm[:, SM_LOGF:SM_LOGF + FOX_HEADS]
    lf_b = jnp.broadcast_to(logf[:, :, None], (Bs, FOX_HEADS, LANES))
    cache_lf_t = cache_logf[layer].transpose(0, 2, 1)
    o_fox = fox_sample_attention(page_table, q, k, v, lf_b, cache_k, cache_v, cache_lf_t, layer)
    x24 = P[:, P_GQKV:P_GQKV + CONV_CH].reshape(Bs, nch, LANES)
    z = P[:, P_Z:P_Z + GDN_VW].reshape(Bs, GDN_HEADS, GDN_DV)
    o_gdn, s_new, conv_new = gdn_sample(
        x24, conv0.reshape(Bs, CONV_W - 1, nch, LANES), pk['conv_w'].reshape(CONV_W, nch, LANES),
        sm.reshape(Bs, 1, LANES), z, pk['gdn_norm_g'], s0)
    x1 = _merge(x, o_fox.reshape(Bs, FOX_W), o_gdn.reshape(Bs, GDN_VW), P, pk, tm=Bs)
    y = _channel_and_ple(x1, p3.reshape(Bs, -1), pk, tm=Bs)
    return (y.reshape(Bs, 1, D), k.reshape(Bs, 1, FOX_HEADS, FOX_HD), v.reshape(Bs, 1, FOX_HEADS, FOX_HD),
            logf.reshape(Bs, 1, FOX_HEADS), s_new, conv_new.reshape(Bs, CONV_W - 1, CONV_CH))


def kernel(x_prompt, x_sample, p_prompt, p_sample, cache_fox_k, cache_fox_v, cache_fox_logf, state_gdn, state_conv, page_table, norm_mix_g, w_in, fox_f_bias, fox_q_norm, fox_k_norm, gdn_conv_w, gdn_a_log, gdn_dt_bias, gdn_norm_g, w_up_fox, w_up_gdn, w_out, norm_ffn_g, peer_w_q, peer_sub_keys, peer_u, peer_v, norm_ple_g, w_ple, w_ple_gate):
    depth = w_in.shape[0]
    D = x_prompt.shape[-1]
    xp, xs = x_prompt, x_sample
    outs = [[] for _ in range(10)]
    for i in range(depth):
        lw = {
            'norm_mix_g': norm_mix_g[i], 'w_in': w_in[i], 'fox_f_bias': fox_f_bias[i],
            'fox_q_norm': fox_q_norm[i], 'fox_k_norm': fox_k_norm[i], 'gdn_conv_w': gdn_conv_w[i],
            'gdn_a_log': gdn_a_log[i], 'gdn_dt_bias': gdn_dt_bias[i], 'gdn_norm_g': gdn_norm_g[i],
            'w_up_fox': w_up_fox[i], 'w_up_gdn': w_up_gdn[i], 'w_out': w_out[i], 'norm_ffn_g': norm_ffn_g[i],
            'peer_w_q': peer_w_q[i], 'peer_sub_keys': peer_sub_keys[i], 'peer_u': peer_u[i], 'peer_v': peer_v[i],
            'norm_ple_g': norm_ple_g[i], 'w_ple': w_ple[i], 'w_ple_gate': w_ple_gate[i],
        }
        pk = _pack_layer(lw, D)
        xp, kp, vp, lfp, sp, cp = _block_prompt(xp, p_prompt[i], pk)
        xs, ks_, vs_, lfs, ss, cs = _block_sample(xs, p_sample[i], cache_fox_k, cache_fox_v, cache_fox_logf,
                                                  state_gdn[i], state_conv[i], page_table, i, pk)
        for lst, val in zip(outs, (kp, vp, lfp, sp, cp, ks_, vs_, lfs, ss, cs)):
            lst.append(val)
    return (xp, xs) + tuple(jnp.stack(o) for o in outs)
```

```python
import functools
import math

import jax
import jax.numpy as jnp
from jax import lax
from jax.experimental import pallas as pl
from jax.experimental.pallas import tpu as pltpu

F32 = jnp.float32
BF16 = jnp.bfloat16
HIGHEST = lax.Precision.HIGHEST
EPS = 1e-6
NEG_BIG = -1e30

LANES = 128
SUBLANES = 8
VMEM_LIMIT = 56 * 1024 * 1024

FOX_HEADS = 8
FOX_HD = 128
FOX_W = FOX_HEADS * FOX_HD
GDN_HEADS = 8
GDN_DK = 128
GDN_DV = 128
GDN_KW = GDN_HEADS * GDN_DK
GDN_VW = GDN_HEADS * GDN_DV
CONV_W = 4
CONV_CH = 2 * GDN_KW + GDN_VW
GDN_CHUNK = 64
PEER_HEADS = 8
N_KEYS = 128
PEER_DK = 128
PEER_TOPK = 16
PEER_SEL = PEER_HEADS * PEER_TOPK

OFF_FQ = 0
OFF_FK = OFF_FQ + FOX_W
OFF_FV = OFF_FK + FOX_W
OFF_FF = OFF_FV + FOX_W
OFF_GQKV = OFF_FF + FOX_HEADS
OFF_GA = OFF_GQKV + CONV_CH
OFF_GB = OFF_GA + GDN_HEADS
OFF_GZ = OFF_GB + GDN_HEADS
OFF_GATE = OFF_GZ + GDN_VW

P_Q = 0
P_K = P_Q + FOX_W
P_V = P_K + FOX_W
P_GQKV = P_V + FOX_W
P_Z = P_GQKV + CONV_CH
P_GATE = P_Z + GDN_VW
SM_LOGF = 0
SM_G = SM_LOGF + FOX_HEADS
SM_BETA = SM_G + GDN_HEADS


def _cparams(sem):
    return pltpu.CompilerParams(dimension_semantics=sem, vmem_limit_bytes=VMEM_LIMIT)


def _sigmoid(x):
    return 1.0 / (1.0 + jnp.exp(-x))


def _silu(x):
    return x * _sigmoid(x)


def _dot(a, b, precision=None):
    return jnp.dot(a, b, preferred_element_type=F32, precision=precision)


def _dot_nt(a, b, precision=None):
    return lax.dot_general(a, b, (((1,), (1,)), ((), ())), preferred_element_type=F32, precision=precision)


def _dot_tn(a, b, precision=None):
    return lax.dot_general(a, b, (((0,), (0,)), ((), ())), preferred_element_type=F32, precision=precision)


def _split(x):
    hi = x.astype(BF16)
    return hi, (x - hi.astype(F32)).astype(BF16)


def _mm3(dot, a, b):
    return dot(a[0], b[0]) + (dot(a[0], b[1]) + dot(a[1], b[0]))


def _normed_linear_kernel(*refs, n_extra, emit_h, epilogue):
    x_ref, g_ref, w_ref = refs[:3]
    extra = refs[3:3 + n_extra]
    out_ref = refs[3 + n_extra]
    h_out = refs[4 + n_extra] if emit_h else None
    h_sc = refs[-1]
    j = pl.program_id(1)

    @pl.when(j == 0)
    def _():
        x = x_ref[...]
        y = x * lax.rsqrt(jnp.mean(x * x, axis=-1, keepdims=True) + EPS) * g_ref[...]
        h_sc[...] = y.astype(BF16)
        if emit_h:
            h_out[...] = y.astype(h_out.dtype)

    acc = _dot(h_sc[...], w_ref[...])
    epilogue(acc, j, extra, out_ref)


def normed_linear(x, g, w, *, tm, tn, epilogue, extra=(), extra_specs=(), out_dtype=F32, emit_h=False):
    T, D = x.shape
    N = w.shape[1]
    tm = min(tm, T)
    tn = min(tn, N)
    assert T % tm == 0 and N % tn == 0
    out_shape = [jax.ShapeDtypeStruct((T, N), out_dtype)]
    out_specs = [pl.BlockSpec((tm, tn), lambda i, j: (i, j))]
    if emit_h:
        out_shape.append(jax.ShapeDtypeStruct((T, D), F32))
        out_specs.append(pl.BlockSpec((tm, D), lambda i, j: (i, 0)))
    res = pl.pallas_call(
        functools.partial(_normed_linear_kernel, n_extra=len(extra), emit_h=emit_h, epilogue=epilogue),
        grid=(T // tm, N // tn),
        in_specs=[pl.BlockSpec((tm, D), lambda i, j: (i, 0)),
                  pl.BlockSpec((1, D), lambda i, j: (0, 0)),
                  pl.BlockSpec((D, tn), lambda i, j: (0, j)),
                  *extra_specs],
        out_specs=out_specs,
        out_shape=out_shape,
        scratch_shapes=[pltpu.VMEM((tm, D), BF16)],
        compiler_params=_cparams(("parallel", "arbitrary")),
    )(x, g, w, *extra)
    return res if emit_h else res[0]


def _plain_epilogue(acc, j, extra, out_ref):
    out_ref[...] = acc.astype(out_ref.dtype)


def _proj_epilogue(acc, j, extra, out_ref, *, tn, n_qk, n_plain, n_gate):
    gain_ref, par_ref = extra

    @pl.when(j < n_qk)
    def _():
        for c in range(tn // LANES):
            blk = acc[:, c * LANES:(c + 1) * LANES]
            ms = jnp.mean(blk * blk, axis=-1, keepdims=True)
            out_ref[:, c * LANES:(c + 1) * LANES] = blk * lax.rsqrt(ms + EPS) * gain_ref[:, c * LANES:(c + 1) * LANES]

    @pl.when(jnp.logical_and(j >= n_qk, j < n_qk + n_plain))
    def _():
        out_ref[...] = acc

    @pl.when(jnp.logical_and(j >= n_qk + n_plain, j < n_qk + n_plain + n_gate))
    def _():
        out_ref[...] = _sigmoid(acc)

    @pl.when(j == n_qk + n_plain + n_gate)
    def _():
        v = acc[:, :LANES] + par_ref[0:1, :]
        lane = lax.broadcasted_iota(jnp.int32, v.shape, 1)
        tail = jnp.log1p(jnp.exp(-jnp.abs(v)))
        logf = -(jnp.maximum(-v, 0.0) + tail)
        g = -jnp.exp(par_ref[1:2, :]) * (jnp.maximum(v, 0.0) + tail)
        beta = _sigmoid(v)
        res = jnp.where(lane < SM_G, logf, jnp.where(lane < SM_BETA, g, beta))
        out_ref[:, :LANES] = res
        if tn > LANES:
            out_ref[:, LANES:] = jnp.zeros((acc.shape[0], tn - LANES), F32)


def _cumsum_kernel(sm_ref, out_ref, carry_sc, *, tr):
    @pl.when(pl.program_id(1) == 0)
    def _():
        carry_sc[...] = jnp.zeros_like(carry_sc)

    r = lax.broadcasted_iota(jnp.int32, (tr, tr), 0)
    c = lax.broadcasted_iota(jnp.int32, (tr, tr), 1)
    tri = (c <= r).astype(F32)
    cs = _dot(tri, sm_ref[...], HIGHEST) + carry_sc[...]
    out_ref[...] = cs
    carry_sc[...] = cs[tr - 1:tr, :]


def seq_cumsum(P, B, L, col_block):
    tr = min(256, L)
    nb = L // tr
    return pl.pallas_call(
        functools.partial(_cumsum_kernel, tr=tr),
        grid=(B, nb),
        in_specs=[pl.BlockSpec((tr, LANES), lambda b, i: (b * nb + i, col_block))],
        out_specs=pl.BlockSpec((tr, LANES), lambda b, i: (b * nb + i, 0)),
        out_shape=jax.ShapeDtypeStruct((B * L, LANES), F32),
        scratch_shapes=[pltpu.VMEM((1, LANES), F32)],
        compiler_params=_cparams(("parallel", "arbitrary")),
    )(P)


def _fox_prompt_kernel(q_ref, k_ref, v_ref, cq_ref, ck_ref, o_ref, m_sc, l_sc, acc_sc, *, tq, tk, scale):
    qi = pl.program_id(1)
    kj = pl.program_id(2)

    @pl.when(kj == 0)
    def _():
        m_sc[...] = jnp.full(m_sc.shape, NEG_BIG, F32)
        l_sc[...] = jnp.zeros_like(l_sc)
        acc_sc[...] = jnp.zeros_like(acc_sc)

    @pl.when(kj * tk <= qi * tq + tq - 1)
    def _():
        q_pos = qi * tq + lax.broadcasted_iota(jnp.int32, (tq, tk), 0)
        k_pos = kj * tk + lax.broadcasted_iota(jnp.int32, (tq, tk), 1)
        keep = k_pos <= q_pos
        cq = cq_ref[...]
        ck = ck_ref[0]
        for h in range(FOX_HEADS):
            sl = slice(h * FOX_HD, (h + 1) * FOX_HD)
            s = _dot_nt(q_ref[:, sl].astype(BF16), k_ref[:, sl].astype(BF16)) * scale
            s = s + (cq[:, h:h + 1] - ck[h:h + 1, :])
            s = jnp.where(keep, s, NEG_BIG)
            m_old = m_sc[h]
            m_new = jnp.maximum(m_old, jnp.max(s, axis=-1, keepdims=True))
            alpha = jnp.exp(m_old - m_new)
            p = jnp.exp(s - m_new)
            l_sc[h] = alpha * l_sc[h] + jnp.sum(p, axis=-1, keepdims=True)
            acc_sc[:, sl] = alpha * acc_sc[:, sl] + _dot(p.astype(BF16), v_ref[:, sl].astype(BF16))
            m_sc[h] = m_new

    @pl.when(kj == pl.num_programs(2) - 1)
    def _():
        for h in range(FOX_HEADS):
            sl = slice(h * FOX_HD, (h + 1) * FOX_HD)
            o_ref[:, sl] = (acc_sc[:, sl] / l_sc[h]).astype(o_ref.dtype)


def fox_prompt_attention(P, c_col, c_row, B, L):
    tq = min(256, L)
    tk = min(512, L)
    nq, nk = L // tq, L // tk

    def kv_block(b, qi, kj):
        last = (qi * tq + tq - 1) // tk
        return b * nk + jnp.minimum(kj, last)

    return pl.pallas_call(
        functools.partial(_fox_prompt_kernel, tq=tq, tk=tk, scale=FOX_HD ** -0.5),
        grid=(B, nq, nk),
        in_specs=[pl.BlockSpec((tq, FOX_W), lambda b, qi, kj: (b * nq + qi, P_Q // FOX_W)),
                  pl.BlockSpec((tk, FOX_W), lambda b, qi, kj: (kv_block(b, qi, kj), P_K // FOX_W)),
                  pl.BlockSpec((tk, FOX_W), lambda b, qi, kj: (kv_block(b, qi, kj), P_V // FOX_W)),
                  pl.BlockSpec((tq, LANES), lambda b, qi, kj: (b * nq + qi, 0)),
                  pl.BlockSpec((1, FOX_HEADS, tk), lambda b, qi, kj: (b, 0, jnp.minimum(kj, (qi * tq + tq - 1) // tk)))],
        out_specs=pl.BlockSpec((tq, FOX_W), lambda b, qi, kj: (b * nq + qi, 0)),
        out_shape=jax.ShapeDtypeStruct((B * L, FOX_W), BF16),
        scratch_shapes=[pltpu.VMEM((FOX_HEADS, tq, 1), F32),
                        pltpu.VMEM((FOX_HEADS, tq, 1), F32),
                        pltpu.VMEM((tq, FOX_W), F32)],
        compiler_params=_cparams(("parallel", "parallel", "arbitrary")),
    )(P, P, P, c_col, c_row)


def _gdn_prep_kernel(x_ref, w_ref, o_ref):
    c = pl.program_id(1)
    x = x_ref[...]
    w = w_ref[...]
    row = lax.broadcasted_iota(jnp.int32, x.shape, 0)
    y = x * w[CONV_W - 1:CONV_W, :]
    for s in range(1, CONV_W):
        shifted = jnp.where(row >= s, pltpu.roll(x, s, 0), 0.0)
        y = y + shifted * w[CONV_W - 1 - s:CONV_W - s, :]
    y = _silu(y)
    normed = y * lax.rsqrt(jnp.sum(y * y, axis=-1, keepdims=True) + EPS)
    q_scale = jnp.where(c < GDN_HEADS, GDN_DK ** -0.5, 1.0)
    o_ref[...] = jnp.where(c < 2 * GDN_HEADS, normed * q_scale, y)


def gdn_prep(P, conv_w, B, L):
    nch = CONV_CH // LANES
    return pl.pallas_call(
        _gdn_prep_kernel,
        grid=(B, nch),
        in_specs=[pl.BlockSpec((L, LANES), lambda b, c: (b, P_GQKV // LANES + c)),
                  pl.BlockSpec((CONV_W, LANES), lambda b, c: (0, c))],
        out_specs=pl.BlockSpec((L, LANES), lambda b, c: (b, c)),
        out_shape=jax.ShapeDtypeStruct((B * L, CONV_CH), F32),
        compiler_params=_cparams(("parallel", "parallel")),
    )(P, conv_w)


def _gdn_out(o, z, gn):
    y = o * lax.rsqrt(jnp.mean(o * o, axis=-1, keepdims=True) + EPS) * gn
    return y * _silu(z)


def _gdn_chunk_kernel(y_ref, sm_ref, z_ref, gn_ref, o_ref, s_out_ref, s_sc):
    ci = pl.program_id(1)
    C = GDN_CHUNK

    @pl.when(ci == 0)
    def _():
        s_sc[...] = jnp.zeros_like(s_sc)

    r = lax.broadcasted_iota(jnp.int32, (C, C), 0)
    c = lax.broadcasted_iota(jnp.int32, (C, C), 1)
    incl = c <= r
    strict = c < r
    eye = (c == r).astype(F32)
    sm = sm_ref[...]
    gcum_all = _dot(incl.astype(F32), sm, HIGHEST)
    lane = lax.broadcasted_iota(jnp.int32, (C, LANES), 1)
    gn = gn_ref[...]
    for h in range(GDN_HEADS):
        q = y_ref[:, h * GDN_DK:(h + 1) * GDN_DK]
        k = y_ref[:, GDN_KW + h * GDN_DK:GDN_KW + (h + 1) * GDN_DK]
        v = y_ref[:, 2 * GDN_KW + h * GDN_DV:2 * GDN_KW + (h + 1) * GDN_DV]
        gc = gcum_all[:, SM_G + h:SM_G + h + 1]
        gr = _dot_nt((lane == SM_G + h).astype(F32), gcum_all, HIGHEST)
        beta = sm[:, SM_BETA + h:SM_BETA + h + 1]
        g_last = gc[C - 1:C, :]
        decay = jnp.where(incl, jnp.exp(jnp.where(incl, gc - gr, 0.0)), 0.0)
        kb = k * beta
        k2 = _split(k)
        a = jnp.where(strict, _mm3(_dot_nt, _split(kb), k2) * decay, 0.0)
        pw = -a
        t_inv = eye + pw
        for _ in range(int(math.log2(C)) - 1):
            pw2 = _split(pw)
            pw = _mm3(_dot, pw2, pw2)
            t_inv = t_inv + _mm3(_dot, _split(t_inv), _split(pw))
        t2 = _split(t_inv)
        u = _mm3(_dot, t2, _split(v * beta))
        w = _mm3(_dot, t2, _split(kb * jnp.exp(gc)))
        attn = _mm3(_dot_nt, _split(q), k2) * decay
        S = s_sc[h]
        S2 = _split(S)
        v_new = u - _mm3(_dot, _split(w), S2)
        v2 = _split(v_new)
        o = _mm3(_dot, _split(q * jnp.exp(gc)), S2) + _mm3(_dot, _split(attn), v2)
        kd = k * jnp.exp(g_last - gc)
        s_sc[h] = S * jnp.exp(g_last) + _mm3(_dot_tn, _split(kd), v2)
        o_ref[:, h * GDN_DV:(h + 1) * GDN_DV] = _gdn_out(o, z_ref[:, h * GDN_DV:(h + 1) * GDN_DV], gn).astype(o_ref.dtype)

    @pl.when(ci == pl.num_programs(1) - 1)
    def _():
        s_out_ref[0] = s_sc[...]


def gdn_chunked(Y, P, gn, B, L, sm_block):
    C = GDN_CHUNK
    n = L // C
    return pl.pallas_call(
        _gdn_chunk_kernel,
        grid=(B, n),
        in_specs=[pl.BlockSpec((C, CONV_CH), lambda b, i: (b * n + i, 0)),
                  pl.BlockSpec((C, LANES), lambda b, i: (b * n + i, sm_block)),
                  pl.BlockSpec((C, GDN_VW), lambda b, i: (b * n + i, P_Z // GDN_VW)),
                  pl.BlockSpec((1, GDN_DV), lambda b, i: (0, 0))],
        out_specs=[pl.BlockSpec((C, GDN_VW), lambda b, i: (b * n + i, 0)),
                   pl.BlockSpec((1, GDN_HEADS, GDN_DK, GDN_DV), lambda b, i: (b, 0, 0, 0))],
        out_shape=[jax.ShapeDtypeStruct((B * L, GDN_VW), BF16),
                   jax.ShapeDtypeStruct((B, GDN_HEADS, GDN_DK, GDN_DV), F32)],
        scratch_shapes=[pltpu.VMEM((GDN_HEADS, GDN_DK, GDN_DV), F32)],
        compiler_params=_cparams(("parallel", "arbitrary")),
    )(Y, P, P, gn)


def _merge_up_kernel(of_ref, og_ref, wf_ref, wg_ref, ga_ref, gb_ref, m_ref):
    m = ga_ref[...] * _dot(of_ref[...], wf_ref[...]) + gb_ref[...] * _dot(og_ref[...], wg_ref[...])
    m_ref[...] = m.astype(m_ref.dtype)


def merge_up(o_fox, o_gdn, w_up_fox, w_up_gdn, P, D, tm, tn):
    T = o_fox.shape[0]
    tm = min(tm, T)
    gate_block = P_GATE // tn
    return pl.pallas_call(
        _merge_up_kernel,
        grid=(T // tm, D // tn),
        in_specs=[pl.BlockSpec((tm, FOX_W), lambda i, j: (i, 0)),
                  pl.BlockSpec((tm, GDN_VW), lambda i, j: (i, 0)),
                  pl.BlockSpec((FOX_W, tn), lambda i, j: (0, j)),
                  pl.BlockSpec((GDN_VW, tn), lambda i, j: (0, j)),
                  pl.BlockSpec((tm, tn), lambda i, j: (i, gate_block + j)),
                  pl.BlockSpec((tm, tn), lambda i, j: (i, gate_block + D // tn + j))],
        out_specs=pl.BlockSpec((tm, tn), lambda i, j: (i, j)),
        out_shape=jax.ShapeDtypeStruct((T, D), BF16),
        compiler_params=_cparams(("parallel", "parallel")),
    )(o_fox, o_gdn, w_up_fox, w_up_gdn, P, P)


def _linear_residual_kernel(a_ref, w_ref, x_ref, o_ref):
    o_ref[...] = x_ref[...] + _dot(a_ref[...], w_ref[...])


def linear_residual(a, w, x, tm, tn):
    T, K = a.shape
    N = w.shape[1]
    tm = min(tm, T)
    return pl.pallas_call(
        _linear_residual_kernel,
        grid=(T // tm, N // tn),
        in_specs=[pl.BlockSpec((tm, K), lambda i, j: (i, 0)),
                  pl.BlockSpec((K, tn), lambda i, j: (0, j)),
                  pl.BlockSpec((tm, tn), lambda i, j: (i, j))],
        out_specs=pl.BlockSpec((tm, tn), lambda i, j: (i, j)),
        out_shape=jax.ShapeDtypeStruct((T, N), F32),
        compiler_params=_cparams(("parallel", "parallel")),
    )(a, w, x)


def _top16(s, val_sc, idx_sc):
    n = s.shape[0]
    row = lax.broadcasted_iota(jnp.int32, s.shape, 0).astype(F32)
    for it in range(PEER_TOPK):
        m = jnp.max(s, axis=0, keepdims=True)
        idx = jnp.min(jnp.where(s == m, row, float(n)), axis=0, keepdims=True)
        val_sc[it:it + 1, :] = m
        idx_sc[it:it + 1, :] = idx
        s = jnp.where(row == idx, -jnp.inf, s)


def _peer_route_kernel(q_ref, keys_ref, idx_ref, gate_ref, va_sc, ia_sc, vb_sc, ib_sc, vc_sc, ic_sc):
    K = PEER_TOPK
    sa = _dot_nt(keys_ref[0, 0], q_ref[:, :PEER_DK])
    _top16(sa, va_sc, ia_sc)
    sb = _dot_nt(keys_ref[0, 1], q_ref[:, PEER_DK:])
    _top16(sb, vb_sc, ib_sc)
    va, vb = va_sc[...], vb_sc[...]
    cand = jnp.concatenate([va[i:i + 1, :] + vb for i in range(K)], axis=0)
    _top16(cand, vc_sc, ic_sc)
    cv, ci = vc_sc[...], ic_sc[...]
    hi = jnp.floor(ci * (1.0 / K))
    lo = ci - hi * K
    ia_all, ib_all = ia_sc[...], ib_sc[...]
    ia = jnp.zeros_like(ci)
    ib = jnp.zeros_like(ci)
    for i in range(K):
        ia = jnp.where(hi == i, ia_all[i:i + 1, :], ia)
        ib = jnp.where(lo == i, ib_all[i:i + 1, :], ib)
    idx_ref[0] = (ia * N_KEYS + ib).astype(jnp.int32)
    e = jnp.exp(cv - cv[0:1, :])
    gate_ref[0] = e / jnp.sum(e, axis=0, keepdims=True)


def peer_route(q, sub_keys, tb):
    T = q.shape[0]
    tb = min(tb, T)
    K = PEER_TOPK
    return pl.pallas_call(
        _peer_route_kernel,
        grid=(T // tb, PEER_HEADS),
        in_specs=[pl.BlockSpec((tb, 2 * PEER_DK), lambda i, h: (i, h)),
                  pl.BlockSpec((1, 2, N_KEYS, PEER_DK), lambda i, h: (h, 0, 0, 0))],
        out_specs=[pl.BlockSpec((1, K, tb), lambda i, h: (h, 0, i)),
                   pl.BlockSpec((1, K, tb), lambda i, h: (h, 0, i))],
        out_shape=[jax.ShapeDtypeStruct((PEER_HEADS, K, T), jnp.int32),
                   jax.ShapeDtypeStruct((PEER_HEADS, K, T), F32)],
        scratch_shapes=[pltpu.VMEM((K, tb), F32) for _ in range(6)],
        compiler_params=_cparams(("parallel", "parallel")),
    )(q, sub_keys)


PEER_TB = 8
PEER_SLOTS = 2
PEER_STEP = PEER_TB * PEER_SLOTS
PEER_ISSUE_UNROLL = 8
GATE_TILE = 128


def _peer_expert_kernel(idx_ref, idx_next_ref, h_ref, gate_ref, x_ref, uv_hbm, o_ref, buf0, buf1, sem, *, D):
    i = pl.program_id(0)
    n = pl.num_programs(0)
    S = 2 * D // LANES
    SU = D // LANES
    bufs = (buf0, buf1)
    slot_rows = PEER_TB * PEER_SEL * S

    def start_fetch(idx, slot):
        for t in range(PEER_TB):
            def body(kb, carry, t=t):
                for kk in range(PEER_ISSUE_UNROLL):
                    k = kb * PEER_ISSUE_UNROLL + kk
                    src = pl.multiple_of(idx[(slot * PEER_TB + t) * PEER_SEL + k] * S, S)
                    dst = pl.multiple_of((t * PEER_SEL + k) * S, S)
                    pltpu.make_async_copy(uv_hbm.at[pl.ds(src, S), :], bufs[slot].at[pl.ds(dst, S), :],
                                          sem.at[slot]).start()
                return carry
            lax.fori_loop(0, PEER_SEL // PEER_ISSUE_UNROLL, body, 0)

    def wait_slot(slot):
        pltpu.make_async_copy(uv_hbm.at[pl.ds(0, slot_rows), :], bufs[slot], sem.at[slot]).wait()

    tok = lax.broadcasted_iota(jnp.int32, (GATE_TILE, PEER_STEP), 0)
    col = lax.broadcasted_iota(jnp.int32, (GATE_TILE, PEER_STEP), 1)
    first = (i % (GATE_TILE // PEER_STEP)) * PEER_STEP
    onehot = (tok == first + col).astype(F32)
    gates = _dot(gate_ref[...], onehot, HIGHEST)

    gr = lax.broadcasted_iota(jnp.int32, (PEER_SEL, SUBLANES * PEER_SEL), 0)
    gc = lax.broadcasted_iota(jnp.int32, (PEER_SEL, SUBLANES * PEER_SEL), 1)
    group_sum = (gc // SUBLANES == gr).astype(BF16)
    sr = lax.broadcasted_iota(jnp.int32, (SUBLANES * PEER_SEL, PEER_SEL), 0)
    sc = lax.broadcasted_iota(jnp.int32, (SUBLANES * PEER_SEL, PEER_SEL), 1)
    spread = (sr // SUBLANES == sc).astype(BF16)
    n_grp = SU // SUBLANES

    def compute(slot):
        buf = bufs[slot]
        for t in range(PEER_TB):
            tt = slot * PEER_TB + t
            blk = buf[t * PEER_SEL * S:(t + 1) * PEER_SEL * S, :].reshape(PEER_SEL, S, LANES)
            h = h_ref[tt:tt + 1, :]
            h2 = jnp.concatenate([h[:, c * LANES:(c + 1) * LANES] for c in range(SU)], axis=0)
            prod = blk[:, :SU, :] * h2[None]
            part = prod[:, :SUBLANES, :]
            for g in range(1, n_grp):
                part = part + prod[:, g * SUBLANES:(g + 1) * SUBLANES, :]
            lane_part = _dot(group_sum, part.reshape(PEER_SEL * SUBLANES, LANES).astype(BF16))
            s = jnp.sum(lane_part, axis=-1, keepdims=True)
            act = 0.5 * s * (1.0 + lax.erf(s * (2.0 ** -0.5)))
            a = act * gates[:, tt:tt + 1]
            a_rep = _dot(spread, jnp.broadcast_to(a, (PEER_SEL, LANES)).astype(BF16))
            a3 = a_rep.reshape(PEER_SEL, SUBLANES, LANES)
            y2 = [jnp.sum(a3 * blk[:, SU + g * SUBLANES:SU + (g + 1) * SUBLANES, :], axis=0) for g in range(n_grp)]
            y = jnp.concatenate([y2[c // SUBLANES][c % SUBLANES:c % SUBLANES + 1, :] for c in range(SU)], axis=1)
            o_ref[tt:tt + 1, :] = x_ref[tt:tt + 1, :] + y

    @pl.when(i == 0)
    def _():
        start_fetch(idx_ref, 0)

    start_fetch(idx_ref, 1)
    wait_slot(0)
    compute(0)

    @pl.when(i + 1 < n)
    def _():
        start_fetch(idx_next_ref, 0)

    wait_slot(1)
    compute(1)


def peer_experts(idx, gate_t, h, x, uv):
    T, D = x.shape
    assert T % PEER_STEP == 0
    n = T // PEER_STEP
    slot_rows = PEER_TB * PEER_SEL * (2 * D // LANES)
    per_tile = GATE_TILE // PEER_STEP
    return pl.pallas_call(
        functools.partial(_peer_expert_kernel, D=D),
        grid=(n,),
        in_specs=[pl.BlockSpec((PEER_STEP * PEER_SEL,), lambda i: (i,), memory_space=pltpu.SMEM),
                  pl.BlockSpec((PEER_STEP * PEER_SEL,), lambda i: (jnp.minimum(i + 1, n - 1),), memory_space=pltpu.SMEM),
                  pl.BlockSpec((PEER_STEP, D), lambda i: (i, 0)),
                  pl.BlockSpec((PEER_SEL, GATE_TILE), lambda i: (0, i // per_tile)),
                  pl.BlockSpec((PEER_STEP, D), lambda i: (i, 0)),
                  pl.BlockSpec(memory_space=pl.ANY)],
        out_specs=pl.BlockSpec((PEER_STEP, D), lambda i: (i, 0)),
        out_shape=jax.ShapeDtypeStruct((T, D), F32),
        scratch_shapes=[pltpu.VMEM((slot_rows, LANES), F32), pltpu.VMEM((slot_rows, LANES), F32),
                        pltpu.SemaphoreType.DMA((PEER_SLOTS,))],
        compiler_params=_cparams(("arbitrary",)),
    )(idx.reshape(-1), idx.reshape(-1), h, gate_t, x, uv)


def _ple_epilogue(acc, j, extra, out_ref):
    x_ref, p_ref, wp_ref = extra
    out_ref[...] = x_ref[...] + _sigmoid(acc) * _dot(p_ref[...], wp_ref[...])


def _fox_sample_kernel(pt_ref, q_ref, ks_ref, vs_ref, lfs_ref, kc_ref, vc_ref, lfc_ref, o_ref,
                       m_sc, l_sc, acc_sc, carry_sc, *, scale):
    j = pl.program_id(1)
    H, PS, HD = FOX_HEADS, kc_ref.shape[0], FOX_HD
    q = q_ref[0]

    @pl.when(j == 0)
    def _():
        m_sc[...] = jnp.sum(q * ks_ref[0], axis=-1, keepdims=True) * scale
        l_sc[...] = jnp.ones_like(l_sc)
        acc_sc[...] = vs_ref[0]
        carry_sc[...] = lfs_ref[0][:, 0:1]

    ones = jnp.ones((HD, LANES), BF16)
    pos3 = lax.broadcasted_iota(jnp.int32, (PS, H, LANES), 0)
    lane3 = lax.broadcasted_iota(jnp.int32, (PS, H, LANES), 2)
    diag = pos3 == lane3
    qk = _dot((kc_ref[...] * q[None]).reshape(PS * H, HD).astype(BF16), ones).reshape(PS, H, LANES)
    s_t = jnp.sum(jnp.where(diag, qk, 0.0), axis=0) * scale
    lf_t = lfc_ref[0]
    jj = lax.broadcasted_iota(jnp.int32, (PS, PS), 0)
    pp = lax.broadcasted_iota(jnp.int32, (PS, PS), 1)
    d_t = _dot(lf_t, (jj > pp).astype(F32), HIGHEST) + carry_sc[...]
    s_t = s_t + d_t
    m_old = m_sc[...]
    m_new = jnp.maximum(m_old, jnp.max(s_t, axis=-1, keepdims=True))
    alpha = jnp.exp(m_old - m_new)
    p_t = jnp.exp(s_t - m_new)
    l_sc[...] = alpha * l_sc[...] + jnp.sum(p_t, axis=-1, keepdims=True)
    p3 = _dot(jnp.where(diag, p_t[None], 0.0).reshape(PS * H, LANES).astype(BF16), ones).reshape(PS, H, LANES)
    acc_sc[...] = alpha * acc_sc[...] + jnp.sum(p3 * vc_ref[...], axis=0)
    m_sc[...] = m_new
    carry_sc[...] = carry_sc[...] + jnp.sum(lf_t, axis=-1, keepdims=True)

    @pl.when(j == pl.num_programs(1) - 1)
    def _():
        o_ref[0] = (acc_sc[...] / l_sc[...]).astype(o_ref.dtype)


def fox_sample_attention(page_table, q, k_self, v_self, lf_self, cache_k, cache_v, cache_lf_t, layer):
    Bs, n_pages = page_table.shape
    PS = cache_k.shape[2]
    H, HD = FOX_HEADS, FOX_HD

    def page(b, j, pt):
        return pt[b, n_pages - 1 - j]

    grid_spec = pltpu.PrefetchScalarGridSpec(
        num_scalar_prefetch=1,
        grid=(Bs, n_pages),
        in_specs=[pl.BlockSpec((1, H, HD), lambda b, j, pt: (b, 0, 0)),
                  pl.BlockSpec((1, H, HD), lambda b, j, pt: (b, 0, 0)),
                  pl.BlockSpec((1, H, HD), lambda b, j, pt: (b, 0, 0)),
                  pl.BlockSpec((1, H, LANES), lambda b, j, pt: (b, 0, 0)),
                  pl.BlockSpec((None, None, PS, H, HD), lambda b, j, pt: (layer, page(b, j, pt), 0, 0, 0)),
                  pl.BlockSpec((None, None, PS, H, HD), lambda b, j, pt: (layer, page(b, j, pt), 0, 0, 0)),
                  pl.BlockSpec((1, H, PS), lambda b, j, pt: (page(b, j, pt), 0, 0))],
        out_specs=pl.BlockSpec((1, H, HD), lambda b, j, pt: (b, 0, 0)),
        scratch_shapes=[pltpu.VMEM((H, 1), F32), pltpu.VMEM((H, 1), F32),
                        pltpu.VMEM((H, HD), F32), pltpu.VMEM((H, 1), F32)],
    )
    return pl.pallas_call(
        functools.partial(_fox_sample_kernel, scale=FOX_HD ** -0.5),
        grid_spec=grid_spec,
        out_shape=jax.ShapeDtypeStruct((Bs, H, HD), BF16),
        compiler_params=_cparams(("parallel", "arbitrary")),
    )(page_table, q, k_self, v_self, lf_self, cache_k, cache_v, cache_lf_t)


def _gdn_sample_kernel(x_ref, cs_ref, w_ref, sm_ref, z_ref, gn_ref, s_ref, o_ref, s_out_ref, conv_out_ref):
    x = x_ref[0]
    cs = cs_ref[0]
    w = w_ref[...]
    y = x * w[CONV_W - 1]
    for jx in range(CONV_W - 1):
        y = y + cs[jx] * w[jx]
    y = _silu(y)
    conv_out_ref[0, 0:CONV_W - 2] = cs[1:]
    conv_out_ref[0, CONV_W - 2] = x
    nh = GDN_HEADS
    normed = y * lax.rsqrt(jnp.sum(y * y, axis=-1, keepdims=True) + EPS)
    rowid = lax.broadcasted_iota(jnp.int32, normed.shape, 0)
    qk = jnp.where(rowid < nh, normed * (GDN_DK ** -0.5), normed)
    pad = jnp.zeros((LANES - 2 * nh, GDN_DK), F32)
    qk_t = jnp.concatenate([qk[:2 * nh], pad], axis=0).T
    sm = sm_ref[0]
    outs = []
    for h in range(nh):
        q_col = qk_t[:, h:h + 1]
        k_col = qk_t[:, nh + h:nh + h + 1]
        v_row = y[2 * nh + h:2 * nh + h + 1, :]
        g = sm[:, SM_G + h:SM_G + h + 1]
        beta = sm[:, SM_BETA + h:SM_BETA + h + 1]
        S = s_ref[0, h] * jnp.exp(g)
        kv = jnp.sum(k_col * S, axis=0, keepdims=True)
        S = S + k_col * ((v_row - kv) * beta)
        s_out_ref[0, h] = S
        outs.append(jnp.sum(q_col * S, axis=0, keepdims=True))
    o = jnp.concatenate(outs, axis=0)
    o_ref[0] = _gdn_out(o, z_ref[0], gn_ref[...]).astype(o_ref.dtype)


def gdn_sample(x24, conv_state, conv_w, sm, z, gn, state):
    Bs = x24.shape[0]
    nch = CONV_CH // LANES
    H = GDN_HEADS
    return pl.pallas_call(
        _gdn_sample_kernel,
        grid=(Bs,),
        in_specs=[pl.BlockSpec((1, nch, LANES), lambda b: (b, 0, 0)),
                  pl.BlockSpec((1, CONV_W - 1, nch, LANES), lambda b: (b, 0, 0, 0)),
                  pl.BlockSpec((CONV_W, nch, LANES), lambda b: (0, 0, 0)),
                  pl.BlockSpec((1, 1, LANES), lambda b: (b, 0, 0)),
                  pl.BlockSpec((1, H, GDN_DV), lambda b: (b, 0, 0)),
                  pl.BlockSpec((1, GDN_DV), lambda b: (0, 0)),
                  pl.BlockSpec((1, H, GDN_DK, GDN_DV), lambda b: (b, 0, 0, 0))],
        out_specs=[pl.BlockSpec((1, H, GDN_DV), lambda b: (b, 0, 0)),
                   pl.BlockSpec((1, H, GDN_DK, GDN_DV), lambda b: (b, 0, 0, 0)),
                   pl.BlockSpec((1, CONV_W - 1, nch, LANES), lambda b: (b, 0, 0, 0))],
        out_shape=[jax.ShapeDtypeStruct((Bs, H, GDN_DV), BF16),
                   jax.ShapeDtypeStruct((Bs, H, GDN_DK, GDN_DV), F32),
                   jax.ShapeDtypeStruct((Bs, CONV_W - 1, nch, LANES), F32)],
        compiler_params=_cparams(("parallel",)),
    )(x24, conv_state, conv_w, sm, z, gn, state)


PROJ_TN = 512


def _pack_layer(lw, D):
    w_in = lw['w_in']
    small = jnp.concatenate([w_in[:, OFF_FF:OFF_GQKV], w_in[:, OFF_GA:OFF_GB], w_in[:, OFF_GB:OFF_GZ]], axis=1)
    small = jnp.pad(small, ((0, 0), (0, PROJ_TN - small.shape[1])))
    w_proj = jnp.concatenate([w_in[:, OFF_FQ:OFF_FF], w_in[:, OFF_GQKV:OFF_GA], w_in[:, OFF_GZ:], small], axis=1).astype(BF16)
    n_main = P_GATE + 2 * D
    gains = jnp.concatenate([jnp.tile(lw['fox_q_norm'], FOX_HEADS), jnp.tile(lw['fox_k_norm'], FOX_HEADS),
                             jnp.zeros((n_main + PROJ_TN - 2 * FOX_W,), F32)])[None, :]
    zeros = jnp.zeros((LANES - SM_BETA,), F32)
    par = jnp.stack([jnp.concatenate([lw['fox_f_bias'], lw['gdn_dt_bias'], zeros]),
                     jnp.concatenate([jnp.zeros((SM_G,), F32), lw['gdn_a_log'], zeros])])
    par = jnp.pad(par, ((0, SUBLANES - 2), (0, 0)))
    return dict(
        w_proj=w_proj, gains=gains, par=par, n_main=n_main,
        norm_mix_g=lw['norm_mix_g'][None, :], conv_w=lw['gdn_conv_w'], gdn_norm_g=lw['gdn_norm_g'][None, :],
        w_up_fox=lw['w_up_fox'].astype(BF16), w_up_gdn=lw['w_up_gdn'].astype(BF16), w_out=lw['w_out'].astype(BF16),
        norm_ffn_g=lw['norm_ffn_g'][None, :], peer_w_q=lw['peer_w_q'].astype(BF16),
        sub_keys=lw['peer_sub_keys'].astype(BF16),
        uv=jnp.concatenate([lw['peer_u'], lw['peer_v']], axis=1).reshape(-1, LANES),
        norm_ple_g=lw['norm_ple_g'][None, :], w_ple=lw['w_ple'].astype(BF16), w_ple_gate=lw['w_ple_gate'].astype(BF16),
    )


def _project(x, pk, tm):
    D = x.shape[1]
    tn = PROJ_TN
    n_qk = 2 * FOX_W // tn
    n_gate = 2 * D // tn
    n_plain = pk['n_main'] // tn - n_qk - n_gate
    epi = functools.partial(_proj_epilogue, tn=tn, n_qk=n_qk, n_plain=n_plain, n_gate=n_gate)
    return normed_linear(
        x, pk['norm_mix_g'], pk['w_proj'], tm=tm, tn=tn, epilogue=epi,
        extra=(pk['gains'], pk['par']),
        extra_specs=(pl.BlockSpec((1, tn), lambda i, j: (0, j)), pl.BlockSpec((SUBLANES, LANES), lambda i, j: (0, 0))))


def _channel_and_ple(x1, p, pk, tm):
    T, D = x1.shape
    Tp = -(-T // GATE_TILE) * GATE_TILE
    q, hn = normed_linear(x1, pk['norm_ffn_g'], pk['peer_w_q'], tm=tm, tn=512, epilogue=_plain_epilogue, out_dtype=BF16, emit_h=True)
    if Tp != T:
        q = jnp.pad(q, ((0, Tp - T), (0, 0)))
    idx_t, gate_t = peer_route(q, pk['sub_keys'], tb=512)
    idx = idx_t.reshape(PEER_SEL, Tp).T[:T]
    x2 = peer_experts(idx, gate_t.reshape(PEER_SEL, Tp), hn, x1, pk['uv'])
    tn = 512
    return normed_linear(
        x2, pk['norm_ple_g'], pk['w_ple_gate'], tm=tm, tn=tn, epilogue=_ple_epilogue,
        extra=(x2, p.astype(BF16), pk['w_ple']),
        extra_specs=(pl.BlockSpec((min(tm, T), tn), lambda i, j: (i, j)),
                     pl.BlockSpec((min(tm, T), p.shape[1]), lambda i, j: (i, 0)),
                     pl.BlockSpec((p.shape[1], tn), lambda i, j: (0, j))))


def _merge(x, o_fox, o_gdn, P, pk, tm):
    D = x.shape[1]
    m = merge_up(o_fox, o_gdn, pk['w_up_fox'], pk['w_up_gdn'], P, D, tm, 512)
    return linear_residual(m, pk['w_out'], x, tm, 512)


def _block_prompt(x3, p3, pk):
    B, L, D = x3.shape
    T = B * L
    x = x3.reshape(T, D)
    sm_block = pk['n_main'] // LANES
    P = _project(x, pk, tm=512)
    c_col = seq_cumsum(P, B, L, sm_block)
    c_row = c_col.reshape(B, L, LANES)[:, :, SM_LOGF:SM_LOGF + FOX_HEADS].transpose(0, 2, 1)
    o_fox = fox_prompt_attention(P, c_col, c_row, B, L)
    Y = gdn_prep(P, pk['conv_w'], B, L)
    o_gdn, s_new = gdn_chunked(Y, P, pk['gdn_norm_g'], B, L, sm_block)
    x1 = _merge(x, o_fox, o_gdn, P, pk, tm=512)
    y = _channel_and_ple(x1, p3.reshape(T, -1), pk, tm=512)
    k = P[:, P_K:P_K + FOX_W].reshape(B, L, FOX_HEADS, FOX_HD)
    v = P[:, P_V:P_V + FOX_W].reshape(B, L, FOX_HEADS, FOX_HD)
    logf = P[:, pk['n_main'] + SM_LOGF:pk['n_main'] + SM_LOGF + FOX_HEADS].reshape(B, L, FOX_HEADS)
    conv_new = P[:, P_GQKV:P_GQKV + CONV_CH].reshape(B, L, CONV_CH)[:, L - (CONV_W - 1):]
    return y.reshape(B, L, D), k, v, logf, s_new, conv_new


def _block_sample(x3, p3, cache_k, cache_v, cache_logf, s0, conv0, page_table, layer, pk):
    Bs, Ls, D = x3.shape
    assert Ls == 1
    x = x3.reshape(Bs, D)
    n_main = pk['n_main']
    nch = CONV_CH // LANES
    P = _project(x, pk, tm=Bs)
    q = P[:, P_Q:P_Q + FOX_W].reshape(Bs, FOX_HEADS, FOX_HD)
    k = P[:, P_K:P_K + FOX_W].reshape(Bs, FOX_HEADS, FOX_HD)
    v = P[:, P_V:P_V + FOX_W].reshape(Bs, FOX_HEADS, FOX_HD)
    sm = P[:, n_main:n_main + LANES]
    logf = sm[:, SM_LOGF:SM_LOGF + FOX_HEADS]
    lf_b = jnp.broadcast_to(logf[:, :, None], (Bs, FOX_HEADS, LANES))
    cache_lf_t = cache_logf[layer].transpose(0, 2, 1)
    o_fox = fox_sample_attention(page_table, q, k, v, lf_b, cache_k, cache_v, cache_lf_t, layer)
    x24 = P[:, P_GQKV:P_GQKV + CONV_CH].reshape(Bs, nch, LANES)
    z = P[:, P_Z:P_Z + GDN_VW].reshape(Bs, GDN_HEADS, GDN_DV)
    o_gdn, s_new, conv_new = gdn_sample(
        x24, conv0.reshape(Bs, CONV_W - 1, nch, LANES), pk['conv_w'].reshape(CONV_W, nch, LANES),
        sm.reshape(Bs, 1, LANES), z, pk['gdn_norm_g'], s0)
    x1 = _merge(x, o_fox.reshape(Bs, FOX_W), o_gdn.reshape(Bs, GDN_VW), P, pk, tm=Bs)
    y = _channel_and_ple(x1, p3.reshape(Bs, -1), pk, tm=Bs)
    return (y.reshape(Bs, 1, D), k.reshape(Bs, 1, FOX_HEADS, FOX_HD), v.reshape(Bs, 1, FOX_HEADS, FOX_HD),
            logf.reshape(Bs, 1, FOX_HEADS), s_new, conv_new.reshape(Bs, CONV_W - 1, CONV_CH))


def kernel(x_prompt, x_sample, p_prompt, p_sample, cache_fox_k, cache_fox_v, cache_fox_logf, state_gdn, state_conv, page_table, norm_mix_g, w_in, fox_f_bias, fox_q_norm, fox_k_norm, gdn_conv_w, gdn_a_log, gdn_dt_bias, gdn_norm_g, w_up_fox, w_up_gdn, w_out, norm_ffn_g, peer_w_q, peer_sub_keys, peer_u, peer_v, norm_ple_g, w_ple, w_ple_gate):
    depth = w_in.shape[0]
    D = x_prompt.shape[-1]
    xp, xs = x_prompt, x_sample
    outs = [[] for _ in range(10)]
    for i in range(depth):
        lw = {
            'norm_mix_g': norm_mix_g[i], 'w_in': w_in[i], 'fox_f_bias': fox_f_bias[i],
            'fox_q_norm': fox_q_norm[i], 'fox_k_norm': fox_k_norm[i], 'gdn_conv_w': gdn_conv_w[i],
            'gdn_a_log': gdn_a_log[i], 'gdn_dt_bias': gdn_dt_bias[i], 'gdn_norm_g': gdn_norm_g[i],
            'w_up_fox': w_up_fox[i], 'w_up_gdn': w_up_gdn[i], 'w_out': w_out[i], 'norm_ffn_g': norm_ffn_g[i],
            'peer_w_q': peer_w_q[i], 'peer_sub_keys': peer_sub_keys[i], 'peer_u': peer_u[i], 'peer_v': peer_v[i],
            'norm_ple_g': norm_ple_g[i], 'w_ple': w_ple[i], 'w_ple_gate': w_ple_gate[i],
        }
        pk = _pack_layer(lw, D)
        xp, kp, vp, lfp, sp, cp = _block_prompt(xp, p_prompt[i], pk)
        xs, ks_, vs_, lfs, ss, cs = _block_sample(xs, p_sample[i], cache_fox_k, cache_fox_v, cache_fox_logf,
                                                  state_gdn[i], state_conv[i], page_table, i, pk)
        for lst, val in zip(outs, (kp, vp, lfp, sp, cp, ks_, vs_, lfs, ss, cs)):
            lst.append(val)
    return (xp, xs) + tuple(jnp.stack(o) for o in outs)
```

```python
import functools
import math

import jax
import jax.numpy as jnp
from jax import lax
from jax.experimental import pallas as pl
from jax.experimental.pallas import tpu as pltpu

F32 = jnp.float32
BF16 = jnp.bfloat16
HIGHEST = lax.Precision.HIGHEST
EPS = 1e-6
NEG_BIG = -1e30

LANES = 128
SUBLANES = 8
MXU_DEPTH = 256
VMEM_LIMIT = 56 * 1024 * 1024

FOX_HEADS = 8
FOX_HD = 128
FOX_W = FOX_HEADS * FOX_HD
GDN_HEADS = 8
GDN_DK = 128
GDN_DV = 128
GDN_KW = GDN_HEADS * GDN_DK
GDN_VW = GDN_HEADS * GDN_DV
CONV_W = 4
CONV_CH = 2 * GDN_KW + GDN_VW
GDN_CHUNK = 64
PEER_HEADS = 8
N_KEYS = 128
PEER_DK = 128
PEER_TOPK = 16
PEER_SEL = PEER_HEADS * PEER_TOPK

OFF_FQ = 0
OFF_FK = OFF_FQ + FOX_W
OFF_FV = OFF_FK + FOX_W
OFF_FF = OFF_FV + FOX_W
OFF_GQKV = OFF_FF + FOX_HEADS
OFF_GA = OFF_GQKV + CONV_CH
OFF_GB = OFF_GA + GDN_HEADS
OFF_GZ = OFF_GB + GDN_HEADS
OFF_GATE = OFF_GZ + GDN_VW

P_Q = 0
P_K = P_Q + FOX_W
P_V = P_K + FOX_W
P_GQKV = P_V + FOX_W
P_Z = P_GQKV + CONV_CH
P_GATE = P_Z + GDN_VW
SM_LOGF = 0
SM_G = SM_LOGF + FOX_HEADS
SM_BETA = SM_G + GDN_HEADS


def _cparams(sem):
    return pltpu.CompilerParams(dimension_semantics=sem, vmem_limit_bytes=VMEM_LIMIT)


def _sigmoid(x):
    return 1.0 / (1.0 + jnp.exp(-x))


def _silu(x):
    return x * _sigmoid(x)


def _dot(a, b, precision=None):
    return jnp.dot(a, b, preferred_element_type=F32, precision=precision)


def _dot_nt(a, b, precision=None):
    return lax.dot_general(a, b, (((1,), (1,)), ((), ())), preferred_element_type=F32, precision=precision)


def _dot_tn(a, b, precision=None):
    return lax.dot_general(a, b, (((0,), (0,)), ((), ())), preferred_element_type=F32, precision=precision)


def _split(x):
    hi = x.astype(BF16)
    return hi, (x - hi.astype(F32)).astype(BF16)


def _mm3(dot, a, b):
    return dot(a[0], b[0]) + (dot(a[0], b[1]) + dot(a[1], b[0]))


def _normed_linear_kernel(*refs, n_extra, emit_h, epilogue):
    x_ref, g_ref, w_ref = refs[:3]
    extra = refs[3:3 + n_extra]
    out_ref = refs[3 + n_extra]
    h_out = refs[4 + n_extra] if emit_h else None
    h_sc = refs[-1]
    j = pl.program_id(1)

    @pl.when(j == 0)
    def _():
        x = x_ref[...]
        y = x * lax.rsqrt(jnp.mean(x * x, axis=-1, keepdims=True) + EPS) * g_ref[...]
        h_sc[...] = y.astype(BF16)
        if emit_h:
            h_out[...] = y.astype(h_out.dtype)

    acc = _dot(h_sc[...], w_ref[...])
    epilogue(acc, j, extra, out_ref)


def normed_linear(x, g, w, *, tm, tn, epilogue, extra=(), extra_specs=(), out_dtype=F32, emit_h=False):
    T, D = x.shape
    N = w.shape[1]
    tm = min(tm, T)
    tn = min(tn, N)
    assert T % tm == 0 and N % tn == 0
    out_shape = [jax.ShapeDtypeStruct((T, N), out_dtype)]
    out_specs = [pl.BlockSpec((tm, tn), lambda i, j: (i, j))]
    if emit_h:
        out_shape.append(jax.ShapeDtypeStruct((T, D), F32))
        out_specs.append(pl.BlockSpec((tm, D), lambda i, j: (i, 0)))
    res = pl.pallas_call(
        functools.partial(_normed_linear_kernel, n_extra=len(extra), emit_h=emit_h, epilogue=epilogue),
        grid=(T // tm, N // tn),
        in_specs=[pl.BlockSpec((tm, D), lambda i, j: (i, 0)),
                  pl.BlockSpec((1, D), lambda i, j: (0, 0)),
                  pl.BlockSpec((D, tn), lambda i, j: (0, j)),
                  *extra_specs],
        out_specs=out_specs,
        out_shape=out_shape,
        scratch_shapes=[pltpu.VMEM((tm, D), BF16)],
        compiler_params=_cparams(("parallel", "arbitrary")),
    )(x, g, w, *extra)
    return res if emit_h else res[0]


def _plain_epilogue(acc, j, extra, out_ref):
    out_ref[...] = acc.astype(out_ref.dtype)


def _proj_epilogue(acc, j, extra, out_ref, *, tn, n_qk, n_plain, n_gate):
    gain_ref, par_ref = extra

    @pl.when(j < n_qk)
    def _():
        for c in range(tn // LANES):
            blk = acc[:, c * LANES:(c + 1) * LANES]
            ms = jnp.mean(blk * blk, axis=-1, keepdims=True)
            out_ref[:, c * LANES:(c + 1) * LANES] = blk * lax.rsqrt(ms + EPS) * gain_ref[:, c * LANES:(c + 1) * LANES]

    @pl.when(jnp.logical_and(j >= n_qk, j < n_qk + n_plain))
    def _():
        out_ref[...] = acc

    @pl.when(jnp.logical_and(j >= n_qk + n_plain, j < n_qk + n_plain + n_gate))
    def _():
        out_ref[...] = _sigmoid(acc)

    @pl.when(j == n_qk + n_plain + n_gate)
    def _():
        v = acc[:, :LANES] + par_ref[0:1, :]
        lane = lax.broadcasted_iota(jnp.int32, v.shape, 1)
        tail = jnp.log1p(jnp.exp(-jnp.abs(v)))
        logf = -(jnp.maximum(-v, 0.0) + tail)
        g = -jnp.exp(par_ref[1:2, :]) * (jnp.maximum(v, 0.0) + tail)
        beta = _sigmoid(v)
        res = jnp.where(lane < SM_G, logf, jnp.where(lane < SM_BETA, g, beta))
        out_ref[:, :LANES] = res
        if tn > LANES:
            out_ref[:, LANES:] = jnp.zeros((acc.shape[0], tn - LANES), F32)


def _cumsum_kernel(sm_ref, out_ref, carry_sc, *, tr):
    @pl.when(pl.program_id(1) == 0)
    def _():
        carry_sc[...] = jnp.zeros_like(carry_sc)

    r = lax.broadcasted_iota(jnp.int32, (tr, tr), 0)
    c = lax.broadcasted_iota(jnp.int32, (tr, tr), 1)
    tri = (c <= r).astype(F32)
    cs = _dot(tri, sm_ref[...], HIGHEST) + carry_sc[...]
    out_ref[...] = cs
    carry_sc[...] = cs[tr - 1:tr, :]


def seq_cumsum(P, B, L, col_block):
    tr = min(256, L)
    nb = L // tr
    return pl.pallas_call(
        functools.partial(_cumsum_kernel, tr=tr),
        grid=(B, nb),
        in_specs=[pl.BlockSpec((tr, LANES), lambda b, i: (b * nb + i, col_block))],
        out_specs=pl.BlockSpec((tr, LANES), lambda b, i: (b * nb + i, 0)),
        out_shape=jax.ShapeDtypeStruct((B * L, LANES), F32),
        scratch_shapes=[pltpu.VMEM((1, LANES), F32)],
        compiler_params=_cparams(("parallel", "arbitrary")),
    )(P)


def _fox_prompt_kernel(q_ref, k_ref, v_ref, cq_ref, ck_ref, o_ref, m_sc, l_sc, acc_sc, *, tq, tk, scale):
    qi = pl.program_id(1)
    kj = pl.program_id(2)

    @pl.when(kj == 0)
    def _():
        m_sc[...] = jnp.full(m_sc.shape, NEG_BIG, F32)
        l_sc[...] = jnp.zeros_like(l_sc)
        acc_sc[...] = jnp.zeros_like(acc_sc)

    @pl.when(kj * tk <= qi * tq + tq - 1)
    def _():
        q_pos = qi * tq + lax.broadcasted_iota(jnp.int32, (tq, tk), 0)
        k_pos = kj * tk + lax.broadcasted_iota(jnp.int32, (tq, tk), 1)
        keep = k_pos <= q_pos
        cq = cq_ref[...]
        ck = ck_ref[0]
        for h in range(FOX_HEADS):
            sl = slice(h * FOX_HD, (h + 1) * FOX_HD)
            s = _dot_nt(q_ref[:, sl].astype(BF16), k_ref[:, sl].astype(BF16)) * scale
            s = s + (cq[:, h:h + 1] - ck[h:h + 1, :])
            s = jnp.where(keep, s, NEG_BIG)
            m_old = m_sc[h]
            m_new = jnp.maximum(m_old, jnp.max(s, axis=-1, keepdims=True))
            alpha = jnp.exp(m_old - m_new)
            p = jnp.exp(s - m_new)
            l_sc[h] = alpha * l_sc[h] + jnp.sum(p, axis=-1, keepdims=True)
            acc_sc[:, sl] = alpha * acc_sc[:, sl] + _dot(p.astype(BF16), v_ref[:, sl].astype(BF16))
            m_sc[h] = m_new

    @pl.when(kj == pl.num_programs(2) - 1)
    def _():
        for h in range(FOX_HEADS):
            sl = slice(h * FOX_HD, (h + 1) * FOX_HD)
            o_ref[:, sl] = (acc_sc[:, sl] / l_sc[h]).astype(o_ref.dtype)


def fox_prompt_attention(P, c_col, c_row, B, L):
    tq = min(256, L)
    tk = min(512, L)
    nq, nk = L // tq, L // tk

    def kv_block(b, qi, kj):
        last = (qi * tq + tq - 1) // tk
        return b * nk + jnp.minimum(kj, last)

    return pl.pallas_call(
        functools.partial(_fox_prompt_kernel, tq=tq, tk=tk, scale=FOX_HD ** -0.5),
        grid=(B, nq, nk),
        in_specs=[pl.BlockSpec((tq, FOX_W), lambda b, qi, kj: (b * nq + qi, P_Q // FOX_W)),
                  pl.BlockSpec((tk, FOX_W), lambda b, qi, kj: (kv_block(b, qi, kj), P_K // FOX_W)),
                  pl.BlockSpec((tk, FOX_W), lambda b, qi, kj: (kv_block(b, qi, kj), P_V // FOX_W)),
                  pl.BlockSpec((tq, LANES), lambda b, qi, kj: (b * nq + qi, 0)),
                  pl.BlockSpec((1, FOX_HEADS, tk), lambda b, qi, kj: (b, 0, jnp.minimum(kj, (qi * tq + tq - 1) // tk)))],
        out_specs=pl.BlockSpec((tq, FOX_W), lambda b, qi, kj: (b * nq + qi, 0)),
        out_shape=jax.ShapeDtypeStruct((B * L, FOX_W), BF16),
        scratch_shapes=[pltpu.VMEM((FOX_HEADS, tq, 1), F32),
                        pltpu.VMEM((FOX_HEADS, tq, 1), F32),
                        pltpu.VMEM((tq, FOX_W), F32)],
        compiler_params=_cparams(("parallel", "parallel", "arbitrary")),
    )(P, P, P, c_col, c_row)


def _gdn_prep_kernel(x_ref, w_ref, o_ref):
    c = pl.program_id(1)
    x = x_ref[...]
    w = w_ref[...]
    row = lax.broadcasted_iota(jnp.int32, x.shape, 0)
    y = x * w[CONV_W - 1:CONV_W, :]
    for s in range(1, CONV_W):
        shifted = jnp.where(row >= s, pltpu.roll(x, s, 0), 0.0)
        y = y + shifted * w[CONV_W - 1 - s:CONV_W - s, :]
    y = _silu(y)
    normed = y * lax.rsqrt(jnp.sum(y * y, axis=-1, keepdims=True) + EPS)
    q_scale = jnp.where(c < GDN_HEADS, GDN_DK ** -0.5, 1.0)
    o_ref[...] = jnp.where(c < 2 * GDN_HEADS, normed * q_scale, y)


def gdn_prep(P, conv_w, B, L):
    nch = CONV_CH // LANES
    return pl.pallas_call(
        _gdn_prep_kernel,
        grid=(B, nch),
        in_specs=[pl.BlockSpec((L, LANES), lambda b, c: (b, P_GQKV // LANES + c)),
                  pl.BlockSpec((CONV_W, LANES), lambda b, c: (0, c))],
        out_specs=pl.BlockSpec((L, LANES), lambda b, c: (b, c)),
        out_shape=jax.ShapeDtypeStruct((B * L, CONV_CH), F32),
        compiler_params=_cparams(("parallel", "parallel")),
    )(P, conv_w)


def _gdn_out(o, z, gn):
    y = o * lax.rsqrt(jnp.mean(o * o, axis=-1, keepdims=True) + EPS) * gn
    return y * _silu(z)


GDN_LOCAL_CHUNKS = 2


def _gdn_local_kernel(y_ref, sm_ref, u_ref, w_ref, qg_ref, kd_ref, attn_ref, eg_ref):
    C = GDN_CHUNK
    assert 2 * C == LANES
    r = lax.broadcasted_iota(jnp.int32, (C, LANES), 0)
    c = lax.broadcasted_iota(jnp.int32, (C, LANES), 1)
    incl = c <= r
    strict = c < r
    r_sq = lax.broadcasted_iota(jnp.int32, (C, C), 0)
    c_sq = lax.broadcasted_iota(jnp.int32, (C, C), 1)
    eye = (c_sq == r_sq).astype(F32)
    tril = (c_sq <= r_sq).astype(F32)
    zpad = jnp.zeros((LANES - C, LANES), F32)
    pw, t_inv, tail = [], [], []
    for ch in range(GDN_LOCAL_CHUNKS):
        rows = slice(ch * C, (ch + 1) * C)
        sm = sm_ref[rows, :]
        gcum_all = _dot(tril, sm, HIGHEST)
        gcum_pad = jnp.concatenate([gcum_all, zpad], axis=0)
        eg_ref[rows, :] = jnp.exp(jnp.broadcast_to(gcum_all[C - 1:C, :], (C, LANES)))
        for h in range(GDN_HEADS):
            hs = slice(h * GDN_DK, (h + 1) * GDN_DK)
            q = y_ref[rows, h * GDN_DK:(h + 1) * GDN_DK]
            k = y_ref[rows, GDN_KW + h * GDN_DK:GDN_KW + (h + 1) * GDN_DK]
            v = y_ref[rows, 2 * GDN_KW + h * GDN_DV:2 * GDN_KW + (h + 1) * GDN_DV]
            gc = gcum_all[:, SM_G + h:SM_G + h + 1]
            gr = _dot_nt((c == SM_G + h).astype(F32), gcum_pad, HIGHEST)
            beta = sm[:, SM_BETA + h:SM_BETA + h + 1]
            g_last = gc[C - 1:C, :]
            decay = jnp.where(incl, jnp.exp(jnp.where(incl, gc - gr, 0.0)), 0.0)
            kb = k * beta
            k2 = _split(jnp.concatenate([k, zpad], axis=0))
            a = jnp.where(strict, _mm3(_dot_nt, _split(kb), k2) * decay, 0.0)[:, :C]
            qg_ref[rows, hs] = q * jnp.exp(gc)
            kd_ref[rows, hs] = k * jnp.exp(g_last - gc)
            attn_ref[rows, h * LANES:(h + 1) * LANES] = _mm3(_dot_nt, _split(q), k2) * decay
            pw.append(-a)
            t_inv.append(eye - a)
            tail.append((rows, hs, _split(v * beta), _split(kb * jnp.exp(gc))))
    for _ in range(int(math.log2(C)) - 1):
        for n in range(len(pw)):
            pw2 = _split(pw[n])
            pw[n] = _mm3(_dot, pw2, pw2)
        for n in range(len(pw)):
            t_inv[n] = t_inv[n] + _mm3(_dot, _split(t_inv[n]), _split(pw[n]))
    for n, (rows, hs, vb2, kg2) in enumerate(tail):
        t2 = _split(t_inv[n])
        u_ref[rows, hs] = _mm3(_dot, t2, vb2)
        w_ref[rows, hs] = _mm3(_dot, t2, kg2)


def _gdn_scan_kernel(u_ref, w_ref, qg_ref, kd_ref, attn_ref, eg_ref, z_ref, gn_ref, o_ref, s_out_ref, s_sc):
    ci = pl.program_id(1)
    C = GDN_CHUNK

    @pl.when(ci == 0)
    def _():
        s_sc[...] = jnp.zeros_like(s_sc)

    gn = gn_ref[...]
    eg_row = eg_ref[...][0:1, :]
    heads = range(GDN_HEADS)
    hs = [slice(h * GDN_DV, (h + 1) * GDN_DV) for h in heads]
    S = [s_sc[h] for h in heads]
    S2 = [_split(S[h]) for h in heads]
    v2 = [_split(u_ref[:, hs[h]] - _mm3(_dot, _split(w_ref[:, hs[h]]), S2[h])) for h in heads]
    o_state = [_mm3(_dot, _split(qg_ref[:, hs[h]]), S2[h]) for h in heads]
    for h in heads:
        o = o_state[h] + _mm3(_dot, _split(attn_ref[:, h * LANES:h * LANES + C]), v2[h])
        o_ref[:, hs[h]] = _gdn_out(o, z_ref[:, hs[h]], gn).astype(o_ref.dtype)
    for h in heads:
        s_sc[h] = S[h] * eg_row[:, SM_G + h:SM_G + h + 1] + _mm3(_dot_tn, _split(kd_ref[:, hs[h]]), v2[h])

    @pl.when(ci == pl.num_programs(1) - 1)
    def _():
        s_out_ref[0] = s_sc[...]


def gdn_chunked(Y, P, gn, B, L, sm_block):
    C = GDN_CHUNK
    n = L // C
    T = B * L
    rows = C * GDN_LOCAL_CHUNKS
    assert T % rows == 0
    wide = jax.ShapeDtypeStruct((T, GDN_VW), F32)
    u, w, qg, kd, attn, eg = pl.pallas_call(
        _gdn_local_kernel,
        grid=(T // rows,),
        in_specs=[pl.BlockSpec((rows, CONV_CH), lambda i: (i, 0)),
                  pl.BlockSpec((rows, LANES), lambda i: (i, sm_block))],
        out_specs=[pl.BlockSpec((rows, GDN_VW), lambda i: (i, 0))] * 4
                  + [pl.BlockSpec((rows, GDN_HEADS * LANES), lambda i: (i, 0)), pl.BlockSpec((rows, LANES), lambda i: (i, 0))],
        out_shape=[wide] * 4 + [jax.ShapeDtypeStruct((T, GDN_HEADS * LANES), F32), jax.ShapeDtypeStruct((T, LANES), F32)],
        compiler_params=_cparams(("parallel",)),
    )(Y, P)
    chunk = lambda width: pl.BlockSpec((C, width), lambda b, i: (b * n + i, 0))
    return pl.pallas_call(
        _gdn_scan_kernel,
        grid=(B, n),
        in_specs=[chunk(GDN_VW), chunk(GDN_VW), chunk(GDN_VW), chunk(GDN_VW), chunk(GDN_HEADS * LANES), chunk(LANES),
                  pl.BlockSpec((C, GDN_VW), lambda b, i: (b * n + i, P_Z // GDN_VW)),
                  pl.BlockSpec((1, GDN_DV), lambda b, i: (0, 0))],
        out_specs=[pl.BlockSpec((C, GDN_VW), lambda b, i: (b * n + i, 0)),
                   pl.BlockSpec((1, GDN_HEADS, GDN_DK, GDN_DV), lambda b, i: (b, 0, 0, 0))],
        out_shape=[jax.ShapeDtypeStruct((T, GDN_VW), BF16),
                   jax.ShapeDtypeStruct((B, GDN_HEADS, GDN_DK, GDN_DV), F32)],
        scratch_shapes=[pltpu.VMEM((GDN_HEADS, GDN_DK, GDN_DV), F32)],
        compiler_params=_cparams(("parallel", "arbitrary")),
    )(u, w, qg, kd, attn, eg, P, gn)


def _merge_up_kernel(of_ref, og_ref, wf_ref, wg_ref, ga_ref, gb_ref, m_ref):
    m = ga_ref[...] * _dot(of_ref[...], wf_ref[...]) + gb_ref[...] * _dot(og_ref[...], wg_ref[...])
    m_ref[...] = m.astype(m_ref.dtype)


def merge_up(o_fox, o_gdn, w_up_fox, w_up_gdn, P, D, tm, tn):
    T = o_fox.shape[0]
    tm = min(tm, T)
    gate_block = P_GATE // tn
    return pl.pallas_call(
        _merge_up_kernel,
        grid=(T // tm, D // tn),
        in_specs=[pl.BlockSpec((tm, FOX_W), lambda i, j: (i, 0)),
                  pl.BlockSpec((tm, GDN_VW), lambda i, j: (i, 0)),
                  pl.BlockSpec((FOX_W, tn), lambda i, j: (0, j)),
                  pl.BlockSpec((GDN_VW, tn), lambda i, j: (0, j)),
                  pl.BlockSpec((tm, tn), lambda i, j: (i, gate_block + j)),
                  pl.BlockSpec((tm, tn), lambda i, j: (i, gate_block + D // tn + j))],
        out_specs=pl.BlockSpec((tm, tn), lambda i, j: (i, j)),
        out_shape=jax.ShapeDtypeStruct((T, D), BF16),
        compiler_params=_cparams(("parallel", "parallel")),
    )(o_fox, o_gdn, w_up_fox, w_up_gdn, P, P)


def _linear_residual_kernel(a_ref, w_ref, x_ref, o_ref):
    o_ref[...] = x_ref[...] + _dot(a_ref[...], w_ref[...])


def linear_residual(a, w, x, tm, tn):
    T, K = a.shape
    N = w.shape[1]
    tm = min(tm, T)
    return pl.pallas_call(
        _linear_residual_kernel,
        grid=(T // tm, N // tn),
        in_specs=[pl.BlockSpec((tm, K), lambda i, j: (i, 0)),
                  pl.BlockSpec((K, tn), lambda i, j: (0, j)),
                  pl.BlockSpec((tm, tn), lambda i, j: (i, j))],
        out_specs=pl.BlockSpec((tm, tn), lambda i, j: (i, j)),
        out_shape=jax.ShapeDtypeStruct((T, N), F32),
        compiler_params=_cparams(("parallel", "parallel")),
    )(a, w, x)


def _top16(s, ids, val_sc, idx_sc):
    for it in range(PEER_TOPK):
        m = jnp.max(s, axis=0, keepdims=True)
        idx = jnp.min(jnp.where(s == m, ids, NO_ID), axis=0, keepdims=True)
        val_sc[it:it + 1, :] = m
        idx_sc[it:it + 1, :] = idx
        s = jnp.where(ids == idx, -jnp.inf, s)


NO_ID = 1e9
PAIR_COUNTS = tuple(PEER_TOPK // (i + 1) for i in range(PEER_TOPK))
PAIR_STARTS = tuple(sum(PAIR_COUNTS[:i]) for i in range(PEER_TOPK))
N_PAIRS = sum(PAIR_COUNTS)
N_PAIR_ROWS = -(-N_PAIRS // SUBLANES) * SUBLANES


def _peer_route_kernel(q_ref, keys_ref, idx_ref, gate_ref, va_sc, ia_sc, vb_sc, ib_sc, vc_sc, ic_sc, cand_sc):
    K = PEER_TOPK
    tb = q_ref.shape[0]
    key_ids = lax.broadcasted_iota(jnp.int32, (N_KEYS, tb), 0).astype(F32)
    sa = _dot_nt(keys_ref[0, 0], q_ref[:, :PEER_DK])
    _top16(sa, key_ids, va_sc, ia_sc)
    sb = _dot_nt(keys_ref[0, 1], q_ref[:, PEER_DK:])
    _top16(sb, key_ids, vb_sc, ib_sc)
    cand_sc[N_PAIR_ROWS - SUBLANES:, :] = jnp.full((SUBLANES, tb), -jnp.inf, F32)
    row = lax.broadcasted_iota(jnp.int32, (N_PAIR_ROWS, tb), 0)
    shift = jnp.zeros((N_PAIR_ROWS, tb), jnp.int32)
    for i in range(K):
        cand_sc[PAIR_STARTS[i]:PAIR_STARTS[i] + PAIR_COUNTS[i], :] = va_sc[i:i + 1, :] + vb_sc[0:PAIR_COUNTS[i], :]
        if i:
            shift = jnp.where(row >= PAIR_STARTS[i], i * K - PAIR_STARTS[i], shift)
    pair_ids = jnp.where(row < N_PAIRS, (row + shift).astype(F32), NO_ID)
    _top16(cand_sc[...], pair_ids, vc_sc, ic_sc)
    cv, ci = vc_sc[...], ic_sc[...]
    hi = jnp.floor(ci * (1.0 / K))
    lo = ci - hi * K
    ia_all, ib_all = ia_sc[...], ib_sc[...]
    ia = jnp.zeros_like(ci)
    ib = jnp.zeros_like(ci)
    for i in range(K):
        ia = jnp.where(hi == i, ia_all[i:i + 1, :], ia)
        ib = jnp.where(lo == i, ib_all[i:i + 1, :], ib)
    idx_ref[0] = (ia * N_KEYS + ib).astype(jnp.int32)
    e = jnp.exp(cv - cv[0:1, :])
    gate_ref[0] = e / jnp.sum(e, axis=0, keepdims=True)


def peer_route(q, sub_keys, tb):
    T = q.shape[0]
    tb = min(tb, T)
    K = PEER_TOPK
    return pl.pallas_call(
        _peer_route_kernel,
        grid=(T // tb, PEER_HEADS),
        in_specs=[pl.BlockSpec((tb, 2 * PEER_DK), lambda i, h: (i, h)),
                  pl.BlockSpec((1, 2, N_KEYS, PEER_DK), lambda i, h: (h, 0, 0, 0))],
        out_specs=[pl.BlockSpec((1, K, tb), lambda i, h: (h, 0, i)),
                   pl.BlockSpec((1, K, tb), lambda i, h: (h, 0, i))],
        out_shape=[jax.ShapeDtypeStruct((PEER_HEADS, K, T), jnp.int32),
                   jax.ShapeDtypeStruct((PEER_HEADS, K, T), F32)],
        scratch_shapes=[pltpu.VMEM((K, tb), F32) for _ in range(6)] + [pltpu.VMEM((N_PAIR_ROWS, tb), F32)],
        compiler_params=_cparams(("parallel", "parallel")),
    )(q, sub_keys)


PEER_TB = 16
PEER_SLOTS = 2
PEER_STEP = PEER_TB * PEER_SLOTS
PEER_ISSUE_UNROLL = 8
PEER_INTERLEAVE = 4
GATE_TILE = 128


def _peer_expert_kernel(idx_ref, idx_next_ref, h_ref, gate_ref, x_ref, uv_hbm, o_ref, buf0, buf1, sem, *, D):
    i = pl.program_id(0)
    n = pl.num_programs(0)
    S = 2 * D // LANES
    SU = D // LANES
    bufs = (buf0, buf1)
    slot_experts = PEER_TB * PEER_SEL

    def start_fetch(idx, slot):
        for t in range(PEER_TB):
            def body(kb, carry, t=t):
                for kk in range(PEER_ISSUE_UNROLL):
                    k = kb * PEER_ISSUE_UNROLL + kk
                    pltpu.make_async_copy(uv_hbm.at[idx[(slot * PEER_TB + t) * PEER_SEL + k]],
                                          bufs[slot].at[t * PEER_SEL + k], sem.at[slot]).start()
                return carry
            lax.fori_loop(0, PEER_SEL // PEER_ISSUE_UNROLL, body, 0)

    def wait_slot(slot):
        pltpu.make_async_copy(uv_hbm.at[pl.ds(0, slot_experts)], bufs[slot], sem.at[slot]).wait()

    tok = lax.broadcasted_iota(jnp.int32, (GATE_TILE, PEER_STEP), 0)
    col = lax.broadcasted_iota(jnp.int32, (GATE_TILE, PEER_STEP), 1)
    first = (i % (GATE_TILE // PEER_STEP)) * PEER_STEP
    onehot = (tok == first + col).astype(F32)
    gates = _dot(gate_ref[...], onehot, HIGHEST)

    def iota2(shape, axis):
        return lax.broadcasted_iota(jnp.int32, shape, axis)

    grp = MXU_DEPTH // SU
    row_group_sum = (iota2((grp, MXU_DEPTH), 1) // SU == iota2((grp, MXU_DEPTH), 0)).astype(BF16)
    lane0 = (iota2((SU, LANES), 1) == 0).astype(BF16)
    expand = (iota2((PEER_SEL, PEER_SEL * SU), 1) // SU == iota2((PEER_SEL, PEER_SEL * SU), 0)).astype(BF16)
    own_row = iota2((SU, PEER_SEL * SU), 1) % SU == iota2((SU, PEER_SEL * SU), 0)

    def compute(slot, next_idx):
        buf = bufs[slot]
        other = 1 - slot
        for t0 in range(0, PEER_TB, PEER_INTERLEAVE):
            toks = range(t0, t0 + PEER_INTERLEAVE)
            for t in toks:
                for k in range(PEER_SEL):
                    pltpu.make_async_copy(uv_hbm.at[next_idx[(other * PEER_TB + t) * PEER_SEL + k]],
                                          bufs[other].at[t * PEER_SEL + k], sem.at[other]).start()
            lane_part = []
            for t in toks:
                tt = slot * PEER_TB + t
                h = h_ref[tt:tt + 1, :]
                h2 = jnp.concatenate([h[:, c * LANES:(c + 1) * LANES] for c in range(SU)], axis=0).astype(BF16)
                prod = (buf[t * PEER_SEL:(t + 1) * PEER_SEL, :SU, :] * h2[None]).reshape(PEER_SEL * SU, LANES)
                lane_part.append(jnp.concatenate(
                    [_dot(row_group_sum, prod[g * MXU_DEPTH:(g + 1) * MXU_DEPTH, :]) for g in range(PEER_SEL // grp)],
                    axis=0))
            a_lanes = []
            for t, lp in zip(toks, lane_part):
                tt = slot * PEER_TB + t
                s = jnp.sum(lp, axis=-1, keepdims=True)
                act = 0.5 * s * (1.0 + lax.erf(s * (2.0 ** -0.5)))
                a = act * gates[:, tt:tt + 1]
                a_lanes.append(_dot_nt(lane0, jnp.broadcast_to(a, (PEER_SEL, LANES)).astype(BF16)))
            a_sel = [jnp.where(own_row, _dot(al.astype(BF16), expand), 0.0).astype(BF16) for al in a_lanes]
            for t, sel in zip(toks, a_sel):
                tt = slot * PEER_TB + t
                v = buf[t * PEER_SEL:(t + 1) * PEER_SEL, SU:, :].reshape(PEER_SEL * SU, LANES)
                y2 = _dot(sel, v)
                y = jnp.concatenate([y2[c:c + 1, :] for c in range(SU)], axis=1)
                o_ref[tt:tt + 1, :] = x_ref[tt:tt + 1, :] + y

    @pl.when(i == 0)
    def _():
        start_fetch(idx_ref, 0)

    wait_slot(0)
    compute(0, idx_ref)
    wait_slot(1)
    compute(1, idx_next_ref)

    @pl.when(i == n - 1)
    def _():
        wait_slot(0)


def peer_experts(idx, gate_t, h, x, uv):
    T, D = x.shape
    assert T % PEER_STEP == 0 and MXU_DEPTH % (D // LANES) == 0
    n = T // PEER_STEP
    slot_shape = (PEER_TB * PEER_SEL, 2 * D // LANES, LANES)
    per_tile = GATE_TILE // PEER_STEP
    return pl.pallas_call(
        functools.partial(_peer_expert_kernel, D=D),
        grid=(n,),
        in_specs=[pl.BlockSpec((PEER_STEP * PEER_SEL,), lambda i: (i,), memory_space=pltpu.SMEM),
                  pl.BlockSpec((PEER_STEP * PEER_SEL,), lambda i: (jnp.minimum(i + 1, n - 1),), memory_space=pltpu.SMEM),
                  pl.BlockSpec((PEER_STEP, D), lambda i: (i, 0)),
                  pl.BlockSpec((PEER_SEL, GATE_TILE), lambda i: (0, i // per_tile)),
                  pl.BlockSpec((PEER_STEP, D), lambda i: (i, 0)),
                  pl.BlockSpec(memory_space=pl.ANY)],
        out_specs=pl.BlockSpec((PEER_STEP, D), lambda i: (i, 0)),
        out_shape=jax.ShapeDtypeStruct((T, D), F32),
        scratch_shapes=[pltpu.VMEM(slot_shape, uv.dtype), pltpu.VMEM(slot_shape, uv.dtype),
                        pltpu.SemaphoreType.DMA((PEER_SLOTS,))],
        compiler_params=_cparams(("arbitrary",)),
    )(idx.reshape(-1), idx.reshape(-1), h, gate_t, x, uv)


def _ple_epilogue(acc, j, extra, out_ref):
    x_ref, p_ref, wp_ref = extra
    out_ref[...] = x_ref[...] + _sigmoid(acc) * _dot(p_ref[...], wp_ref[...])


def _fox_sample_kernel(pt_ref, q_ref, ks_ref, vs_ref, lfs_ref, kc_ref, vc_ref, lfc_ref, o_ref,
                       m_sc, l_sc, acc_sc, carry_sc, *, scale):
    j = pl.program_id(1)
    H, PS, HD = FOX_HEADS, kc_ref.shape[0], FOX_HD
    q = q_ref[0]

    @pl.when(j == 0)
    def _():
        m_sc[...] = jnp.sum(q * ks_ref[0], axis=-1, keepdims=True) * scale
        l_sc[...] = jnp.ones_like(l_sc)
        acc_sc[...] = vs_ref[0]
        carry_sc[...] = lfs_ref[0][:, 0:1]

    ones = jnp.ones((HD, LANES), BF16)
    pos3 = lax.broadcasted_iota(jnp.int32, (PS, H, LANES), 0)
    lane3 = lax.broadcasted_iota(jnp.int32, (PS, H, LANES), 2)
    diag = pos3 == lane3
    qk = _dot((kc_ref[...] * q[None]).reshape(PS * H, HD).astype(BF16), ones).reshape(PS, H, LANES)
    s_t = jnp.sum(jnp.where(diag, qk, 0.0), axis=0) * scale
    lf_t = lfc_ref[0]
    jj = lax.broadcasted_iota(jnp.int32, (PS, PS), 0)
    pp = lax.broadcasted_iota(jnp.int32, (PS, PS), 1)
    d_t = _dot(lf_t, (jj > pp).astype(F32), HIGHEST) + carry_sc[...]
    s_t = s_t + d_t
    m_old = m_sc[...]
    m_new = jnp.maximum(m_old, jnp.max(s_t, axis=-1, keepdims=True))
    alpha = jnp.exp(m_old - m_new)
    p_t = jnp.exp(s_t - m_new)
    l_sc[...] = alpha * l_sc[...] + jnp.sum(p_t, axis=-1, keepdims=True)
    p3 = _dot(jnp.where(diag, p_t[None], 0.0).reshape(PS * H, LANES).astype(BF16), ones).reshape(PS, H, LANES)
    acc_sc[...] = alpha * acc_sc[...] + jnp.sum(p3 * vc_ref[...], axis=0)
    m_sc[...] = m_new
    carry_sc[...] = carry_sc[...] + jnp.sum(lf_t, axis=-1, keepdims=True)

    @pl.when(j == pl.num_programs(1) - 1)
    def _():
        o_ref[0] = (acc_sc[...] / l_sc[...]).astype(o_ref.dtype)


def fox_sample_attention(page_table, q, k_self, v_self, lf_self, cache_k, cache_v, cache_lf_t, layer):
    Bs, n_pages = page_table.shape
    PS = cache_k.shape[2]
    H, HD = FOX_HEADS, FOX_HD

    def page(b, j, pt):
        return pt[b, n_pages - 1 - j]

    grid_spec = pltpu.PrefetchScalarGridSpec(
        num_scalar_prefetch=1,
        grid=(Bs, n_pages),
        in_specs=[pl.BlockSpec((1, H, HD), lambda b, j, pt: (b, 0, 0)),
                  pl.BlockSpec((1, H, HD), lambda b, j, pt: (b, 0, 0)),
                  pl.BlockSpec((1, H, HD), lambda b, j, pt: (b, 0, 0)),
                  pl.BlockSpec((1, H, LANES), lambda b, j, pt: (b, 0, 0)),
                  pl.BlockSpec((None, None, PS, H, HD), lambda b, j, pt: (layer, page(b, j, pt), 0, 0, 0)),
                  pl.BlockSpec((None, None, PS, H, HD), lambda b, j, pt: (layer, page(b, j, pt), 0, 0, 0)),
                  pl.BlockSpec((1, H, PS), lambda b, j, pt: (page(b, j, pt), 0, 0))],
        out_specs=pl.BlockSpec((1, H, HD), lambda b, j, pt: (b, 0, 0)),
        scratch_shapes=[pltpu.VMEM((H, 1), F32), pltpu.VMEM((H, 1), F32),
                        pltpu.VMEM((H, HD), F32), pltpu.VMEM((H, 1), F32)],
    )
    return pl.pallas_call(
        functools.partial(_fox_sample_kernel, scale=FOX_HD ** -0.5),
        grid_spec=grid_spec,
        out_shape=jax.ShapeDtypeStruct((Bs, H, HD), BF16),
        compiler_params=_cparams(("parallel", "arbitrary")),
    )(page_table, q, k_self, v_self, lf_self, cache_k, cache_v, cache_lf_t)


def _gdn_sample_kernel(x_ref, cs_ref, w_ref, sm_ref, z_ref, gn_ref, s_ref, o_ref, s_out_ref, conv_out_ref):
    x = x_ref[0]
    cs = cs_ref[0]
    w = w_ref[...]
    y = x * w[CONV_W - 1]
    for jx in range(CONV_W - 1):
        y = y + cs[jx] * w[jx]
    y = _silu(y)
    conv_out_ref[0, 0:CONV_W - 2] = cs[1:]
    conv_out_ref[0, CONV_W - 2] = x
    nh = GDN_HEADS
    normed = y * lax.rsqrt(jnp.sum(y * y, axis=-1, keepdims=True) + EPS)
    rowid = lax.broadcasted_iota(jnp.int32, normed.shape, 0)
    qk = jnp.where(rowid < nh, normed * (GDN_DK ** -0.5), normed)
    pad = jnp.zeros((LANES - 2 * nh, GDN_DK), F32)
    qk_t = jnp.concatenate([qk[:2 * nh], pad], axis=0).T
    sm = sm_ref[0]
    outs = []
    for h in range(nh):
        q_col = qk_t[:, h:h + 1]
        k_col = qk_t[:, nh + h:nh + h + 1]
        v_row = y[2 * nh + h:2 * nh + h + 1, :]
        g = sm[:, SM_G + h:SM_G + h + 1]
        beta = sm[:, SM_BETA + h:SM_BETA + h + 1]
        S = s_ref[0, h] * jnp.exp(g)
        kv = jnp.sum(k_col * S, axis=0, keepdims=True)
        S = S + k_col * ((v_row - kv) * beta)
        s_out_ref[0, h] = S
        outs.append(jnp.sum(q_col * S, axis=0, keepdims=True))
    o = jnp.concatenate(outs, axis=0)
    o_ref[0] = _gdn_out(o, z_ref[0], gn_ref[...]).astype(o_ref.dtype)


def gdn_sample(x24, conv_state, conv_w, sm, z, gn, state):
    Bs = x24.shape[0]
    nch = CONV_CH // LANES
    H = GDN_HEADS
    return pl.pallas_call(
        _gdn_sample_kernel,
        grid=(Bs,),
        in_specs=[pl.BlockSpec((1, nch, LANES), lambda b: (b, 0, 0)),
                  pl.BlockSpec((1, CONV_W - 1, nch, LANES), lambda b: (b, 0, 0, 0)),
                  pl.BlockSpec((CONV_W, nch, LANES), lambda b: (0, 0, 0)),
                  pl.BlockSpec((1, 1, LANES), lambda b: (b, 0, 0)),
                  pl.BlockSpec((1, H, GDN_DV), lambda b: (b, 0, 0)),
                  pl.BlockSpec((1, GDN_DV), lambda b: (0, 0)),
                  pl.BlockSpec((1, H, GDN_DK, GDN_DV), lambda b: (b, 0, 0, 0))],
        out_specs=[pl.BlockSpec((1, H, GDN_DV), lambda b: (b, 0, 0)),
                   pl.BlockSpec((1, H, GDN_DK, GDN_DV), lambda b: (b, 0, 0, 0)),
                   pl.BlockSpec((1, CONV_W - 1, nch, LANES), lambda b: (b, 0, 0, 0))],
        out_shape=[jax.ShapeDtypeStruct((Bs, H, GDN_DV), BF16),
                   jax.ShapeDtypeStruct((Bs, H, GDN_DK, GDN_DV), F32),
                   jax.ShapeDtypeStruct((Bs, CONV_W - 1, nch, LANES), F32)],
        compiler_params=_cparams(("parallel",)),
    )(x24, conv_state, conv_w, sm, z, gn, state)


PROJ_TN = 512


def _pack_layer(lw, D):
    w_in = lw['w_in']
    small = jnp.concatenate([w_in[:, OFF_FF:OFF_GQKV], w_in[:, OFF_GA:OFF_GB], w_in[:, OFF_GB:OFF_GZ]], axis=1)
    small = jnp.pad(small, ((0, 0), (0, PROJ_TN - small.shape[1])))
    w_proj = jnp.concatenate([w_in[:, OFF_FQ:OFF_FF], w_in[:, OFF_GQKV:OFF_GA], w_in[:, OFF_GZ:], small], axis=1).astype(BF16)
    n_main = P_GATE + 2 * D
    gains = jnp.concatenate([jnp.tile(lw['fox_q_norm'], FOX_HEADS), jnp.tile(lw['fox_k_norm'], FOX_HEADS),
                             jnp.zeros((n_main + PROJ_TN - 2 * FOX_W,), F32)])[None, :]
    zeros = jnp.zeros((LANES - SM_BETA,), F32)
    par = jnp.stack([jnp.concatenate([lw['fox_f_bias'], lw['gdn_dt_bias'], zeros]),
                     jnp.concatenate([jnp.zeros((SM_G,), F32), lw['gdn_a_log'], zeros])])
    par = jnp.pad(par, ((0, SUBLANES - 2), (0, 0)))
    return dict(
        w_proj=w_proj, gains=gains, par=par, n_main=n_main,
        norm_mix_g=lw['norm_mix_g'][None, :], conv_w=lw['gdn_conv_w'], gdn_norm_g=lw['gdn_norm_g'][None, :],
        w_up_fox=lw['w_up_fox'].astype(BF16), w_up_gdn=lw['w_up_gdn'].astype(BF16), w_out=lw['w_out'].astype(BF16),
        norm_ffn_g=lw['norm_ffn_g'][None, :], peer_w_q=lw['peer_w_q'].astype(BF16),
        sub_keys=lw['peer_sub_keys'].astype(BF16),
        uv=jnp.concatenate([lw['peer_u'].astype(BF16), lw['peer_v'].astype(BF16)], axis=1).reshape(-1, 2 * D // LANES, LANES),
        norm_ple_g=lw['norm_ple_g'][None, :], w_ple=lw['w_ple'].astype(BF16), w_ple_gate=lw['w_ple_gate'].astype(BF16),
    )


def _project(x, pk, tm):
    D = x.shape[1]
    tn = PROJ_TN
    n_qk = 2 * FOX_W // tn
    n_gate = 2 * D // tn
    n_plain = pk['n_main'] // tn - n_qk - n_gate
    epi = functools.partial(_proj_epilogue, tn=tn, n_qk=n_qk, n_plain=n_plain, n_gate=n_gate)
    return normed_linear(
        x, pk['norm_mix_g'], pk['w_proj'], tm=tm, tn=tn, epilogue=epi,
        extra=(pk['gains'], pk['par']),
        extra_specs=(pl.BlockSpec((1, tn), lambda i, j: (0, j)), pl.BlockSpec((SUBLANES, LANES), lambda i, j: (0, 0))))


def _channel_and_ple(x1, p, pk, tm):
    T, D = x1.shape
    Tp = -(-T // GATE_TILE) * GATE_TILE
    q, hn = normed_linear(x1, pk['norm_ffn_g'], pk['peer_w_q'], tm=tm, tn=512, epilogue=_plain_epilogue, out_dtype=BF16, emit_h=True)
    if Tp != T:
        q = jnp.pad(q, ((0, Tp - T), (0, 0)))
    idx_t, gate_t = peer_route(q, pk['sub_keys'], tb=512)
    idx = idx_t.reshape(PEER_SEL, Tp).T[:T]
    x2 = peer_experts(idx, gate_t.reshape(PEER_SEL, Tp), hn, x1, pk['uv'])
    tn = 512
    return normed_linear(
        x2, pk['norm_ple_g'], pk['w_ple_gate'], tm=tm, tn=tn, epilogue=_ple_epilogue,
        extra=(x2, p.astype(BF16), pk['w_ple']),
        extra_specs=(pl.BlockSpec((min(tm, T), tn), lambda i, j: (i, j)),
                     pl.BlockSpec((min(tm, T), p.shape[1]), lambda i, j: (i, 0)),
                     pl.BlockSpec((p.shape[1], tn), lambda i, j: (0, j))))


def _merge(x, o_fox, o_gdn, P, pk, tm):
    D = x.shape[1]
    m = merge_up(o_fox, o_gdn, pk['w_up_fox'], pk['w_up_gdn'], P, D, tm, 512)
    return linear_residual(m, pk['w_out'], x, tm, 512)


def _block_prompt(x3, p3, pk):
    B, L, D = x3.shape
    T = B * L
    x = x3.reshape(T, D)
    sm_block = pk['n_main'] // LANES
    P = _project(x, pk, tm=512)
    c_col = seq_cumsum(P, B, L, sm_block)
    c_row = c_col.reshape(B, L, LANES)[:, :, SM_LOGF:SM_LOGF + FOX_HEADS].transpose(0, 2, 1)
    o_fox = fox_prompt_attention(P, c_col, c_row, B, L)
    Y = gdn_prep(P, pk['conv_w'], B, L)
    o_gdn, s_new = gdn_chunked(Y, P, pk['gdn_norm_g'], B, L, sm_block)
    x1 = _merge(x, o_fox, o_gdn, P, pk, tm=512)
    y = _channel_and_ple(x1, p3.reshape(T, -1), pk, tm=512)
    k = P[:, P_K:P_K + FOX_W].reshape(B, L, FOX_HEADS, FOX_HD)
    v = P[:, P_V:P_V + FOX_W].reshape(B, L, FOX_HEADS, FOX_HD)
    logf = P[:, pk['n_main'] + SM_LOGF:pk['n_main'] + SM_LOGF + FOX_HEADS].reshape(B, L, FOX_HEADS)
    conv_new = P[:, P_GQKV:P_GQKV + CONV_CH].reshape(B, L, CONV_CH)[:, L - (CONV_W - 1):]
    return y.reshape(B, L, D), k, v, logf, s_new, conv_new


def _block_sample(x3, p3, cache_k, cache_v, cache_logf, s0, conv0, page_table, layer, pk):
    Bs, Ls, D = x3.shape
    assert Ls == 1
    x = x3.reshape(Bs, D)
    n_main = pk['n_main']
    nch = CONV_CH // LANES
    P = _project(x, pk, tm=Bs)
    q = P[:, P_Q:P_Q + FOX_W].reshape(Bs, FOX_HEADS, FOX_HD)
    k = P[:, P_K:P_K + FOX_W].reshape(Bs, FOX_HEADS, FOX_HD)
    v = P[:, P_V:P_V + FOX_W].reshape(Bs, FOX_HEADS, FOX_HD)
    sm = P[:, n_main:n_main + LANES]
    logf = sm[:, SM_LOGF:SM_LOGF + FOX_HEADS]
    lf_b = jnp.broadcast_to(logf[:, :, None], (Bs, FOX_HEADS, LANES))
    cache_lf_t = cache_logf[layer].transpose(0, 2, 1)
    o_fox = fox_sample_attention(page_table, q, k, v, lf_b, cache_k, cache_v, cache_lf_t, layer)
    x24 = P[:, P_GQKV:P_GQKV + CONV_CH].reshape(Bs, nch, LANES)
    z = P[:, P_Z:P_Z + GDN_VW].reshape(Bs, GDN_HEADS, GDN_DV)
    o_gdn, s_new, conv_new = gdn_sample(
        x24, conv0.reshape(Bs, CONV_W - 1, nch, LANES), pk['conv_w'].reshape(CONV_W, nch, LANES),
        sm.reshape(Bs, 1, LANES), z, pk['gdn_norm_g'], s0)
    x1 = _merge(x, o_fox.reshape(Bs, FOX_W), o_gdn.reshape(Bs, GDN_VW), P, pk, tm=Bs)
    y = _channel_and_ple(x1, p3.reshape(Bs, -1), pk, tm=Bs)
    return (y.reshape(Bs, 1, D), k.reshape(Bs, 1, FOX_HEADS, FOX_HD), v.reshape(Bs, 1, FOX_HEADS, FOX_HD),
            logf.reshape(Bs, 1, FOX_HEADS), s_new, conv_new.reshape(Bs, CONV_W - 1, CONV_CH))


def kernel(x_prompt, x_sample, p_prompt, p_sample, cache_fox_k, cache_fox_v, cache_fox_logf, state_gdn, state_conv, page_table, norm_mix_g, w_in, fox_f_bias, fox_q_norm, fox_k_norm, gdn_conv_w, gdn_a_log, gdn_dt_bias, gdn_norm_g, w_up_fox, w_up_gdn, w_out, norm_ffn_g, peer_w_q, peer_sub_keys, peer_u, peer_v, norm_ple_g, w_ple, w_ple_gate):
    depth = w_in.shape[0]
    D = x_prompt.shape[-1]
    xp, xs = x_prompt, x_sample
    outs = [[] for _ in range(10)]
    for i in range(depth):
        lw = {
            'norm_mix_g': norm_mix_g[i], 'w_in': w_in[i], 'fox_f_bias': fox_f_bias[i],
            'fox_q_norm': fox_q_norm[i], 'fox_k_norm': fox_k_norm[i], 'gdn_conv_w': gdn_conv_w[i],
            'gdn_a_log': gdn_a_log[i], 'gdn_dt_bias': gdn_dt_bias[i], 'gdn_norm_g': gdn_norm_g[i],
            'w_up_fox': w_up_fox[i], 'w_up_gdn': w_up_gdn[i], 'w_out': w_out[i], 'norm_ffn_g': norm_ffn_g[i],
            'peer_w_q': peer_w_q[i], 'peer_sub_keys': peer_sub_keys[i], 'peer_u': peer_u[i], 'peer_v': peer_v[i],
            'norm_ple_g': norm_ple_g[i], 'w_ple': w_ple[i], 'w_ple_gate': w_ple_gate[i],
        }
        pk = _pack_layer(lw, D)
        xp, kp, vp, lfp, sp, cp = _block_prompt(xp, p_prompt[i], pk)
        xs, ks_, vs_, lfs, ss, cs = _block_sample(xs, p_sample[i], cache_fox_k, cache_fox_v, cache_fox_logf,
                                                  state_gdn[i], state_conv[i], page_table, i, pk)
        for lst, val in zip(outs, (kp, vp, lfp, sp, cp, ks_, vs_, lfs, ss, cs)):
            lst.append(val)
    return (xp, xs) + tuple(jnp.stack(o) for o in outs)
```

```python
import functools
import math

import jax
import jax.numpy as jnp
from jax import lax
from jax.experimental import pallas as pl
from jax.experimental.pallas import tpu as pltpu

F32 = jnp.float32
BF16 = jnp.bfloat16
HIGHEST = lax.Precision.HIGHEST
EPS = 1e-6
NEG_BIG = -1e30

LANES = 128
SUBLANES = 8
MXU_DEPTH = 256
DMA_THREADS = 2
VMEM_LIMIT = 56 * 1024 * 1024

FOX_HEADS = 8
FOX_HD = 128
FOX_W = FOX_HEADS * FOX_HD
GDN_HEADS = 8
GDN_DK = 128
GDN_DV = 128
GDN_KW = GDN_HEADS * GDN_DK
GDN_VW = GDN_HEADS * GDN_DV
CONV_W = 4
CONV_CH = 2 * GDN_KW + GDN_VW
GDN_CHUNK = 64
PEER_HEADS = 8
N_KEYS = 128
PEER_DK = 128
PEER_TOPK = 16
PEER_SEL = PEER_HEADS * PEER_TOPK

OFF_FQ = 0
OFF_FK = OFF_FQ + FOX_W
OFF_FV = OFF_FK + FOX_W
OFF_FF = OFF_FV + FOX_W
OFF_GQKV = OFF_FF + FOX_HEADS
OFF_GA = OFF_GQKV + CONV_CH
OFF_GB = OFF_GA + GDN_HEADS
OFF_GZ = OFF_GB + GDN_HEADS
OFF_GATE = OFF_GZ + GDN_VW

P_Q = 0
P_K = P_Q + FOX_W
P_V = P_K + FOX_W
P_GQKV = P_V + FOX_W
P_Z = P_GQKV + CONV_CH
P_GATE = P_Z + GDN_VW
SM_LOGF = 0
SM_G = SM_LOGF + FOX_HEADS
SM_BETA = SM_G + GDN_HEADS


def _cparams(sem):
    return pltpu.CompilerParams(dimension_semantics=sem, vmem_limit_bytes=VMEM_LIMIT)


def _sigmoid(x):
    return 1.0 / (1.0 + jnp.exp(-x))


def _silu(x):
    return x * _sigmoid(x)


def _dot(a, b, precision=None):
    return jnp.dot(a, b, preferred_element_type=F32, precision=precision)


def _dot_nt(a, b, precision=None):
    return lax.dot_general(a, b, (((1,), (1,)), ((), ())), preferred_element_type=F32, precision=precision)


def _dot_tn(a, b, precision=None):
    return lax.dot_general(a, b, (((0,), (0,)), ((), ())), preferred_element_type=F32, precision=precision)


def _split(x):
    hi = x.astype(BF16)
    return hi, (x - hi.astype(F32)).astype(BF16)


def _mm3(dot, a, b):
    return dot(a[0], b[0]) + (dot(a[0], b[1]) + dot(a[1], b[0]))


def _normed_linear_kernel(*refs, n_extra, emit_h, epilogue):
    x_ref, g_ref, w_ref = refs[:3]
    extra = refs[3:3 + n_extra]
    out_ref = refs[3 + n_extra]
    h_out = refs[4 + n_extra] if emit_h else None
    h_sc = refs[-1]
    j = pl.program_id(1)

    @pl.when(j == 0)
    def _():
        x = x_ref[...]
        y = x * lax.rsqrt(jnp.mean(x * x, axis=-1, keepdims=True) + EPS) * g_ref[...]
        h_sc[...] = y.astype(BF16)
        if emit_h:
            h_out[...] = y.astype(h_out.dtype)

    acc = _dot(h_sc[...], w_ref[...])
    epilogue(acc, j, extra, out_ref)


def normed_linear(x, g, w, *, tm, tn, epilogue, extra=(), extra_specs=(), out_dtype=F32, emit_h=False):
    T, D = x.shape
    N = w.shape[1]
    tm = min(tm, T)
    tn = min(tn, N)
    assert T % tm == 0 and N % tn == 0
    out_shape = [jax.ShapeDtypeStruct((T, N), out_dtype)]
    out_specs = [pl.BlockSpec((tm, tn), lambda i, j: (i, j))]
    if emit_h:
        out_shape.append(jax.ShapeDtypeStruct((T, D), F32))
        out_specs.append(pl.BlockSpec((tm, D), lambda i, j: (i, 0)))
    res = pl.pallas_call(
        functools.partial(_normed_linear_kernel, n_extra=len(extra), emit_h=emit_h, epilogue=epilogue),
        grid=(T // tm, N // tn),
        in_specs=[pl.BlockSpec((tm, D), lambda i, j: (i, 0)),
                  pl.BlockSpec((1, D), lambda i, j: (0, 0)),
                  pl.BlockSpec((D, tn), lambda i, j: (0, j)),
                  *extra_specs],
        out_specs=out_specs,
        out_shape=out_shape,
        scratch_shapes=[pltpu.VMEM((tm, D), BF16)],
        compiler_params=_cparams(("parallel", "arbitrary")),
    )(x, g, w, *extra)
    return res if emit_h else res[0]


def _plain_epilogue(acc, j, extra, out_ref):
    out_ref[...] = acc.astype(out_ref.dtype)


def _proj_epilogue(acc, j, extra, out_ref, *, tn, n_qk, n_plain, n_gate):
    gain_ref, par_ref = extra

    @pl.when(j < n_qk)
    def _():
        for c in range(tn // LANES):
            blk = acc[:, c * LANES:(c + 1) * LANES]
            ms = jnp.mean(blk * blk, axis=-1, keepdims=True)
            out_ref[:, c * LANES:(c + 1) * LANES] = blk * lax.rsqrt(ms + EPS) * gain_ref[:, c * LANES:(c + 1) * LANES]

    @pl.when(jnp.logical_and(j >= n_qk, j < n_qk + n_plain))
    def _():
        out_ref[...] = acc

    @pl.when(jnp.logical_and(j >= n_qk + n_plain, j < n_qk + n_plain + n_gate))
    def _():
        out_ref[...] = _sigmoid(acc)

    @pl.when(j == n_qk + n_plain + n_gate)
    def _():
        v = acc[:, :LANES] + par_ref[0:1, :]
        lane = lax.broadcasted_iota(jnp.int32, v.shape, 1)
        tail = jnp.log1p(jnp.exp(-jnp.abs(v)))
        logf = -(jnp.maximum(-v, 0.0) + tail)
        g = -jnp.exp(par_ref[1:2, :]) * (jnp.maximum(v, 0.0) + tail)
        beta = _sigmoid(v)
        res = jnp.where(lane < SM_G, logf, jnp.where(lane < SM_BETA, g, beta))
        out_ref[:, :LANES] = res
        if tn > LANES:
            out_ref[:, LANES:] = jnp.zeros((acc.shape[0], tn - LANES), F32)


def _cumsum_kernel(sm_ref, out_ref, carry_sc, *, tr):
    @pl.when(pl.program_id(1) == 0)
    def _():
        carry_sc[...] = jnp.zeros_like(carry_sc)

    r = lax.broadcasted_iota(jnp.int32, (tr, tr), 0)
    c = lax.broadcasted_iota(jnp.int32, (tr, tr), 1)
    tri = (c <= r).astype(F32)
    cs = _dot(tri, sm_ref[...], HIGHEST) + carry_sc[...]
    out_ref[...] = cs
    carry_sc[...] = cs[tr - 1:tr, :]


def seq_cumsum(P, B, L, col_block):
    tr = min(256, L)
    nb = L // tr
    return pl.pallas_call(
        functools.partial(_cumsum_kernel, tr=tr),
        grid=(B, nb),
        in_specs=[pl.BlockSpec((tr, LANES), lambda b, i: (b * nb + i, col_block))],
        out_specs=pl.BlockSpec((tr, LANES), lambda b, i: (b * nb + i, 0)),
        out_shape=jax.ShapeDtypeStruct((B * L, LANES), F32),
        scratch_shapes=[pltpu.VMEM((1, LANES), F32)],
        compiler_params=_cparams(("parallel", "arbitrary")),
    )(P)


def _fox_prompt_kernel(q_ref, k_ref, v_ref, cq_ref, ck_ref, o_ref, m_sc, l_sc, acc_sc, *, tq, tk, scale):
    qi = pl.program_id(1)
    kj = pl.program_id(2)

    @pl.when(kj == 0)
    def _():
        m_sc[...] = jnp.full(m_sc.shape, NEG_BIG, F32)
        l_sc[...] = jnp.zeros_like(l_sc)
        acc_sc[...] = jnp.zeros_like(acc_sc)

    @pl.when(kj * tk <= qi * tq + tq - 1)
    def _():
        q_pos = qi * tq + lax.broadcasted_iota(jnp.int32, (tq, tk), 0)
        k_pos = kj * tk + lax.broadcasted_iota(jnp.int32, (tq, tk), 1)
        keep = k_pos <= q_pos
        cq = cq_ref[...]
        ck = ck_ref[0]
        for h in range(FOX_HEADS):
            sl = slice(h * FOX_HD, (h + 1) * FOX_HD)
            s = _dot_nt(q_ref[:, sl].astype(BF16), k_ref[:, sl].astype(BF16)) * scale
            s = s + (cq[:, h:h + 1] - ck[h:h + 1, :])
            s = jnp.where(keep, s, NEG_BIG)
            m_old = m_sc[h]
            m_new = jnp.maximum(m_old, jnp.max(s, axis=-1, keepdims=True))
            alpha = jnp.exp(m_old - m_new)
            p = jnp.exp(s - m_new)
            l_sc[h] = alpha * l_sc[h] + jnp.sum(p, axis=-1, keepdims=True)
            acc_sc[:, sl] = alpha * acc_sc[:, sl] + _dot(p.astype(BF16), v_ref[:, sl].astype(BF16))
            m_sc[h] = m_new

    @pl.when(kj == pl.num_programs(2) - 1)
    def _():
        for h in range(FOX_HEADS):
            sl = slice(h * FOX_HD, (h + 1) * FOX_HD)
            o_ref[:, sl] = (acc_sc[:, sl] / l_sc[h]).astype(o_ref.dtype)


def fox_prompt_attention(P, c_col, c_row, B, L):
    tq = min(256, L)
    tk = min(512, L)
    nq, nk = L // tq, L // tk

    def kv_block(b, qi, kj):
        last = (qi * tq + tq - 1) // tk
        return b * nk + jnp.minimum(kj, last)

    return pl.pallas_call(
        functools.partial(_fox_prompt_kernel, tq=tq, tk=tk, scale=FOX_HD ** -0.5),
        grid=(B, nq, nk),
        in_specs=[pl.BlockSpec((tq, FOX_W), lambda b, qi, kj: (b * nq + qi, P_Q // FOX_W)),
                  pl.BlockSpec((tk, FOX_W), lambda b, qi, kj: (kv_block(b, qi, kj), P_K // FOX_W)),
                  pl.BlockSpec((tk, FOX_W), lambda b, qi, kj: (kv_block(b, qi, kj), P_V // FOX_W)),
                  pl.BlockSpec((tq, LANES), lambda b, qi, kj: (b * nq + qi, 0)),
                  pl.BlockSpec((1, FOX_HEADS, tk), lambda b, qi, kj: (b, 0, jnp.minimum(kj, (qi * tq + tq - 1) // tk)))],
        out_specs=pl.BlockSpec((tq, FOX_W), lambda b, qi, kj: (b * nq + qi, 0)),
        out_shape=jax.ShapeDtypeStruct((B * L, FOX_W), BF16),
        scratch_shapes=[pltpu.VMEM((FOX_HEADS, tq, 1), F32),
                        pltpu.VMEM((FOX_HEADS, tq, 1), F32),
                        pltpu.VMEM((tq, FOX_W), F32)],
        compiler_params=_cparams(("parallel", "parallel", "arbitrary")),
    )(P, P, P, c_col, c_row)


def _gdn_prep_kernel(x_ref, w_ref, o_ref):
    c = pl.program_id(1)
    x = x_ref[...]
    w = w_ref[...]
    row = lax.broadcasted_iota(jnp.int32, x.shape, 0)
    y = x * w[CONV_W - 1:CONV_W, :]
    for s in range(1, CONV_W):
        shifted = jnp.where(row >= s, pltpu.roll(x, s, 0), 0.0)
        y = y + shifted * w[CONV_W - 1 - s:CONV_W - s, :]
    y = _silu(y)
    normed = y * lax.rsqrt(jnp.sum(y * y, axis=-1, keepdims=True) + EPS)
    q_scale = jnp.where(c < GDN_HEADS, GDN_DK ** -0.5, 1.0)
    o_ref[...] = jnp.where(c < 2 * GDN_HEADS, normed * q_scale, y)


def gdn_prep(P, conv_w, B, L):
    nch = CONV_CH // LANES
    return pl.pallas_call(
        _gdn_prep_kernel,
        grid=(B, nch),
        in_specs=[pl.BlockSpec((L, LANES), lambda b, c: (b, P_GQKV // LANES + c)),
                  pl.BlockSpec((CONV_W, LANES), lambda b, c: (0, c))],
        out_specs=pl.BlockSpec((L, LANES), lambda b, c: (b, c)),
        out_shape=jax.ShapeDtypeStruct((B * L, CONV_CH), F32),
        compiler_params=_cparams(("parallel", "parallel")),
    )(P, conv_w)


def _gdn_out(o, z, gn):
    y = o * lax.rsqrt(jnp.mean(o * o, axis=-1, keepdims=True) + EPS) * gn
    return y * _silu(z)


GDN_LOCAL_CHUNKS = 2


def _gdn_local_kernel(y_ref, sm_ref, u_ref, w_ref, qg_ref, kd_ref, attn_ref, eg_ref):
    C = GDN_CHUNK
    assert 2 * C == LANES
    r = lax.broadcasted_iota(jnp.int32, (C, LANES), 0)
    c = lax.broadcasted_iota(jnp.int32, (C, LANES), 1)
    incl = c <= r
    strict = c < r
    r_sq = lax.broadcasted_iota(jnp.int32, (C, C), 0)
    c_sq = lax.broadcasted_iota(jnp.int32, (C, C), 1)
    eye = (c_sq == r_sq).astype(F32)
    tril = (c_sq <= r_sq).astype(F32)
    zpad = jnp.zeros((LANES - C, LANES), F32)
    pw, t_inv, tail = [], [], []
    for ch in range(GDN_LOCAL_CHUNKS):
        rows = slice(ch * C, (ch + 1) * C)
        sm = sm_ref[rows, :]
        gcum_all = _dot(tril, sm, HIGHEST)
        gcum_pad = jnp.concatenate([gcum_all, zpad], axis=0)
        eg_ref[rows, :] = jnp.exp(jnp.broadcast_to(gcum_all[C - 1:C, :], (C, LANES)))
        for h in range(GDN_HEADS):
            hs = slice(h * GDN_DK, (h + 1) * GDN_DK)
            q = y_ref[rows, h * GDN_DK:(h + 1) * GDN_DK]
            k = y_ref[rows, GDN_KW + h * GDN_DK:GDN_KW + (h + 1) * GDN_DK]
            v = y_ref[rows, 2 * GDN_KW + h * GDN_DV:2 * GDN_KW + (h + 1) * GDN_DV]
            gc = gcum_all[:, SM_G + h:SM_G + h + 1]
            gr = _dot_nt((c == SM_G + h).astype(F32), gcum_pad, HIGHEST)
            beta = sm[:, SM_BETA + h:SM_BETA + h + 1]
            g_last = gc[C - 1:C, :]
            decay = jnp.where(incl, jnp.exp(jnp.where(incl, gc - gr, 0.0)), 0.0)
            kb = k * beta
            k2 = _split(jnp.concatenate([k, zpad], axis=0))
            a = jnp.where(strict, _mm3(_dot_nt, _split(kb), k2) * decay, 0.0)[:, :C]
            qg_ref[rows, hs] = q * jnp.exp(gc)
            kd_ref[rows, hs] = k * jnp.exp(g_last - gc)
            attn_ref[rows, h * LANES:(h + 1) * LANES] = _mm3(_dot_nt, _split(q), k2) * decay
            pw.append(-a)
            t_inv.append(eye - a)
            tail.append((rows, hs, _split(v * beta), _split(kb * jnp.exp(gc))))
    for _ in range(int(math.log2(C)) - 1):
        for n in range(len(pw)):
            pw2 = _split(pw[n])
            pw[n] = _mm3(_dot, pw2, pw2)
        for n in range(len(pw)):
            t_inv[n] = t_inv[n] + _mm3(_dot, _split(t_inv[n]), _split(pw[n]))
    for n, (rows, hs, vb2, kg2) in enumerate(tail):
        t2 = _split(t_inv[n])
        u_ref[rows, hs] = _mm3(_dot, t2, vb2)
        w_ref[rows, hs] = _mm3(_dot, t2, kg2)


def _gdn_scan_kernel(u_ref, w_ref, qg_ref, kd_ref, attn_ref, eg_ref, z_ref, gn_ref, o_ref, s_out_ref, s_sc):
    ci = pl.program_id(1)
    C = GDN_CHUNK

    @pl.when(ci == 0)
    def _():
        s_sc[...] = jnp.zeros_like(s_sc)

    gn = gn_ref[...]
    eg_row = eg_ref[...][0:1, :]
    heads = range(GDN_HEADS)
    hs = [slice(h * GDN_DV, (h + 1) * GDN_DV) for h in heads]
    S = [s_sc[h] for h in heads]
    S2 = [_split(S[h]) for h in heads]
    v2 = [_split(u_ref[:, hs[h]] - _mm3(_dot, _split(w_ref[:, hs[h]]), S2[h])) for h in heads]
    o_state = [_mm3(_dot, _split(qg_ref[:, hs[h]]), S2[h]) for h in heads]
    for h in heads:
        o = o_state[h] + _mm3(_dot, _split(attn_ref[:, h * LANES:h * LANES + C]), v2[h])
        o_ref[:, hs[h]] = _gdn_out(o, z_ref[:, hs[h]], gn).astype(o_ref.dtype)
    for h in heads:
        s_sc[h] = S[h] * eg_row[:, SM_G + h:SM_G + h + 1] + _mm3(_dot_tn, _split(kd_ref[:, hs[h]]), v2[h])

    @pl.when(ci == pl.num_programs(1) - 1)
    def _():
        s_out_ref[0] = s_sc[...]


def gdn_chunked(Y, P, gn, B, L, sm_block):
    C = GDN_CHUNK
    n = L // C
    T = B * L
    rows = C * GDN_LOCAL_CHUNKS
    assert T % rows == 0
    wide = jax.ShapeDtypeStruct((T, GDN_VW), F32)
    u, w, qg, kd, attn, eg = pl.pallas_call(
        _gdn_local_kernel,
        grid=(T // rows,),
        in_specs=[pl.BlockSpec((rows, CONV_CH), lambda i: (i, 0)),
                  pl.BlockSpec((rows, LANES), lambda i: (i, sm_block))],
        out_specs=[pl.BlockSpec((rows, GDN_VW), lambda i: (i, 0))] * 4
                  + [pl.BlockSpec((rows, GDN_HEADS * LANES), lambda i: (i, 0)), pl.BlockSpec((rows, LANES), lambda i: (i, 0))],
        out_shape=[wide] * 4 + [jax.ShapeDtypeStruct((T, GDN_HEADS * LANES), F32), jax.ShapeDtypeStruct((T, LANES), F32)],
        compiler_params=_cparams(("parallel",)),
    )(Y, P)
    chunk = lambda width: pl.BlockSpec((C, width), lambda b, i: (b * n + i, 0))
    return pl.pallas_call(
        _gdn_scan_kernel,
        grid=(B, n),
        in_specs=[chunk(GDN_VW), chunk(GDN_VW), chunk(GDN_VW), chunk(GDN_VW), chunk(GDN_HEADS * LANES), chunk(LANES),
                  pl.BlockSpec((C, GDN_VW), lambda b, i: (b * n + i, P_Z // GDN_VW)),
                  pl.BlockSpec((1, GDN_DV), lambda b, i: (0, 0))],
        out_specs=[pl.BlockSpec((C, GDN_VW), lambda b, i: (b * n + i, 0)),
                   pl.BlockSpec((1, GDN_HEADS, GDN_DK, GDN_DV), lambda b, i: (b, 0, 0, 0))],
        out_shape=[jax.ShapeDtypeStruct((T, GDN_VW), BF16),
                   jax.ShapeDtypeStruct((B, GDN_HEADS, GDN_DK, GDN_DV), F32)],
        scratch_shapes=[pltpu.VMEM((GDN_HEADS, GDN_DK, GDN_DV), F32)],
        compiler_params=_cparams(("parallel", "arbitrary")),
    )(u, w, qg, kd, attn, eg, P, gn)


def _merge_up_kernel(of_ref, og_ref, wf_ref, wg_ref, ga_ref, gb_ref, m_ref):
    m = ga_ref[...] * _dot(of_ref[...], wf_ref[...]) + gb_ref[...] * _dot(og_ref[...], wg_ref[...])
    m_ref[...] = m.astype(m_ref.dtype)


def merge_up(o_fox, o_gdn, w_up_fox, w_up_gdn, P, D, tm, tn):
    T = o_fox.shape[0]
    tm = min(tm, T)
    gate_block = P_GATE // tn
    return pl.pallas_call(
        _merge_up_kernel,
        grid=(T // tm, D // tn),
        in_specs=[pl.BlockSpec((tm, FOX_W), lambda i, j: (i, 0)),
                  pl.BlockSpec((tm, GDN_VW), lambda i, j: (i, 0)),
                  pl.BlockSpec((FOX_W, tn), lambda i, j: (0, j)),
                  pl.BlockSpec((GDN_VW, tn), lambda i, j: (0, j)),
                  pl.BlockSpec((tm, tn), lambda i, j: (i, gate_block + j)),
                  pl.BlockSpec((tm, tn), lambda i, j: (i, gate_block + D // tn + j))],
        out_specs=pl.BlockSpec((tm, tn), lambda i, j: (i, j)),
        out_shape=jax.ShapeDtypeStruct((T, D), BF16),
        compiler_params=_cparams(("parallel", "parallel")),
    )(o_fox, o_gdn, w_up_fox, w_up_gdn, P, P)


def _linear_residual_kernel(a_ref, w_ref, x_ref, o_ref):
    o_ref[...] = x_ref[...] + _dot(a_ref[...], w_ref[...])


def linear_residual(a, w, x, tm, tn):
    T, K = a.shape
    N = w.shape[1]
    tm = min(tm, T)
    return pl.pallas_call(
        _linear_residual_kernel,
        grid=(T // tm, N // tn),
        in_specs=[pl.BlockSpec((tm, K), lambda i, j: (i, 0)),
                  pl.BlockSpec((K, tn), lambda i, j: (0, j)),
                  pl.BlockSpec((tm, tn), lambda i, j: (i, j))],
        out_specs=pl.BlockSpec((tm, tn), lambda i, j: (i, j)),
        out_shape=jax.ShapeDtypeStruct((T, N), F32),
        compiler_params=_cparams(("parallel", "parallel")),
    )(a, w, x)


def _top16(s, ids, val_sc, idx_sc):
    for it in range(PEER_TOPK):
        m = jnp.max(s, axis=0, keepdims=True)
        idx = jnp.min(jnp.where(s == m, ids, NO_ID), axis=0, keepdims=True)
        val_sc[it:it + 1, :] = m
        idx_sc[it:it + 1, :] = idx
        s = jnp.where(ids == idx, -jnp.inf, s)


NO_ID = 1e9
PAIR_COUNTS = tuple(PEER_TOPK // (i + 1) for i in range(PEER_TOPK))
PAIR_STARTS = tuple(sum(PAIR_COUNTS[:i]) for i in range(PEER_TOPK))
N_PAIRS = sum(PAIR_COUNTS)
N_PAIR_ROWS = -(-N_PAIRS // SUBLANES) * SUBLANES


def _peer_route_kernel(q_ref, keys_ref, idx_ref, gate_ref, va_sc, ia_sc, vb_sc, ib_sc, vc_sc, ic_sc, cand_sc):
    K = PEER_TOPK
    tb = q_ref.shape[0]
    key_ids = lax.broadcasted_iota(jnp.int32, (N_KEYS, tb), 0).astype(F32)
    sa = _dot_nt(keys_ref[0, 0], q_ref[:, :PEER_DK])
    _top16(sa, key_ids, va_sc, ia_sc)
    sb = _dot_nt(keys_ref[0, 1], q_ref[:, PEER_DK:])
    _top16(sb, key_ids, vb_sc, ib_sc)
    cand_sc[N_PAIR_ROWS - SUBLANES:, :] = jnp.full((SUBLANES, tb), -jnp.inf, F32)
    row = lax.broadcasted_iota(jnp.int32, (N_PAIR_ROWS, tb), 0)
    shift = jnp.zeros((N_PAIR_ROWS, tb), jnp.int32)
    for i in range(K):
        cand_sc[PAIR_STARTS[i]:PAIR_STARTS[i] + PAIR_COUNTS[i], :] = va_sc[i:i + 1, :] + vb_sc[0:PAIR_COUNTS[i], :]
        if i:
            shift = jnp.where(row >= PAIR_STARTS[i], i * K - PAIR_STARTS[i], shift)
    pair_ids = jnp.where(row < N_PAIRS, (row + shift).astype(F32), NO_ID)
    _top16(cand_sc[...], pair_ids, vc_sc, ic_sc)
    cv, ci = vc_sc[...], ic_sc[...]
    hi = jnp.floor(ci * (1.0 / K))
    lo = ci - hi * K
    ia_all, ib_all = ia_sc[...], ib_sc[...]
    ia = jnp.zeros_like(ci)
    ib = jnp.zeros_like(ci)
    for i in range(K):
        ia = jnp.where(hi == i, ia_all[i:i + 1, :], ia)
        ib = jnp.where(lo == i, ib_all[i:i + 1, :], ib)
    idx_ref[0] = (ia * N_KEYS + ib).astype(jnp.int32)
    e = jnp.exp(cv - cv[0:1, :])
    gate_ref[0] = e / jnp.sum(e, axis=0, keepdims=True)


def peer_route(q, sub_keys, tb):
    T = q.shape[0]
    tb = min(tb, T)
    K = PEER_TOPK
    return pl.pallas_call(
        _peer_route_kernel,
        grid=(T // tb, PEER_HEADS),
        in_specs=[pl.BlockSpec((tb, 2 * PEER_DK), lambda i, h: (i, h)),
                  pl.BlockSpec((1, 2, N_KEYS, PEER_DK), lambda i, h: (h, 0, 0, 0))],
        out_specs=[pl.BlockSpec((1, K, tb), lambda i, h: (h, 0, i)),
                   pl.BlockSpec((1, K, tb), lambda i, h: (h, 0, i))],
        out_shape=[jax.ShapeDtypeStruct((PEER_HEADS, K, T), jnp.int32),
                   jax.ShapeDtypeStruct((PEER_HEADS, K, T), F32)],
        scratch_shapes=[pltpu.VMEM((K, tb), F32) for _ in range(6)] + [pltpu.VMEM((N_PAIR_ROWS, tb), F32)],
        compiler_params=_cparams(("parallel", "parallel")),
    )(q, sub_keys)


PEER_TB = 16
PEER_SLOTS = 2
PEER_STEP = PEER_TB * PEER_SLOTS
PEER_ISSUE_UNROLL = 8
PEER_INTERLEAVE = 4
GATE_TILE = 128


def _peer_expert_kernel(idx_ref, idx_next_ref, h_ref, gate_ref, x_ref, uv_hbm, o_ref, buf0, buf1, sem, *, D):
    i = pl.program_id(0)
    n = pl.num_programs(0)
    S = 2 * D // LANES
    SU = D // LANES
    bufs = (buf0, buf1)
    slot_experts = PEER_TB * PEER_SEL

    def start_fetch(idx, slot):
        for t in range(PEER_TB):
            def body(kb, carry, t=t):
                for kk in range(PEER_ISSUE_UNROLL):
                    k = kb * PEER_ISSUE_UNROLL + kk
                    pltpu.make_async_copy(uv_hbm.at[idx[(slot * PEER_TB + t) * PEER_SEL + k]],
                                          bufs[slot].at[t * PEER_SEL + k], sem.at[slot]).start()
                return carry
            lax.fori_loop(0, PEER_SEL // PEER_ISSUE_UNROLL, body, 0)

    def wait_slot(slot):
        pltpu.make_async_copy(uv_hbm.at[pl.ds(0, slot_experts)], bufs[slot], sem.at[slot]).wait()

    tok = lax.broadcasted_iota(jnp.int32, (GATE_TILE, PEER_STEP), 0)
    col = lax.broadcasted_iota(jnp.int32, (GATE_TILE, PEER_STEP), 1)
    first = (i % (GATE_TILE // PEER_STEP)) * PEER_STEP
    onehot = (tok == first + col).astype(F32)
    gates = _dot(gate_ref[...], onehot, HIGHEST)

    def iota2(shape, axis):
        return lax.broadcasted_iota(jnp.int32, shape, axis)

    grp = MXU_DEPTH // SU
    row_group_sum = (iota2((grp, MXU_DEPTH), 1) // SU == iota2((grp, MXU_DEPTH), 0)).astype(BF16)
    lane0 = (iota2((SU, LANES), 1) == 0).astype(BF16)
    expand = (iota2((PEER_SEL, PEER_SEL * SU), 1) // SU == iota2((PEER_SEL, PEER_SEL * SU), 0)).astype(BF16)
    own_row = iota2((SU, PEER_SEL * SU), 1) % SU == iota2((SU, PEER_SEL * SU), 0)

    def compute(slot, next_idx):
        buf = bufs[slot]
        other = 1 - slot
        for t0 in range(0, PEER_TB, PEER_INTERLEAVE):
            toks = range(t0, t0 + PEER_INTERLEAVE)
            for t in toks:
                for k in range(PEER_SEL):
                    pltpu.make_async_copy(uv_hbm.at[next_idx[(other * PEER_TB + t) * PEER_SEL + k]],
                                          bufs[other].at[t * PEER_SEL + k], sem.at[other]).start(
                                              priority=k % DMA_THREADS)
            lane_part = []
            for t in toks:
                tt = slot * PEER_TB + t
                h = h_ref[tt:tt + 1, :]
                h2 = jnp.concatenate([h[:, c * LANES:(c + 1) * LANES] for c in range(SU)], axis=0).astype(BF16)
                prod = (buf[t * PEER_SEL:(t + 1) * PEER_SEL, :SU, :] * h2[None]).reshape(PEER_SEL * SU, LANES)
                lane_part.append(jnp.concatenate(
                    [_dot(row_group_sum, prod[g * MXU_DEPTH:(g + 1) * MXU_DEPTH, :]) for g in range(PEER_SEL // grp)],
                    axis=0))
            a_lanes = []
            for t, lp in zip(toks, lane_part):
                tt = slot * PEER_TB + t
                s = jnp.sum(lp, axis=-1, keepdims=True)
                act = 0.5 * s * (1.0 + lax.erf(s * (2.0 ** -0.5)))
                a = act * gates[:, tt:tt + 1]
                a_lanes.append(_dot_nt(lane0, jnp.broadcast_to(a, (PEER_SEL, LANES)).astype(BF16)))
            a_sel = [jnp.where(own_row, _dot(al.astype(BF16), expand), 0.0).astype(BF16) for al in a_lanes]
            for t, sel in zip(toks, a_sel):
                tt = slot * PEER_TB + t
                v = buf[t * PEER_SEL:(t + 1) * PEER_SEL, SU:, :].reshape(PEER_SEL * SU, LANES)
                y2 = _dot(sel, v)
                y = jnp.concatenate([y2[c:c + 1, :] for c in range(SU)], axis=1)
                o_ref[tt:tt + 1, :] = x_ref[tt:tt + 1, :] + y

    @pl.when(i == 0)
    def _():
        start_fetch(idx_ref, 0)

    wait_slot(0)
    compute(0, idx_ref)
    wait_slot(1)
    compute(1, idx_next_ref)

    @pl.when(i == n - 1)
    def _():
        wait_slot(0)


def peer_experts(idx, gate_t, h, x, uv):
    T, D = x.shape
    assert T % PEER_STEP == 0 and MXU_DEPTH % (D // LANES) == 0
    n = T // PEER_STEP
    slot_shape = (PEER_TB * PEER_SEL, 2 * D // LANES, LANES)
    per_tile = GATE_TILE // PEER_STEP
    return pl.pallas_call(
        functools.partial(_peer_expert_kernel, D=D),
        grid=(n,),
        in_specs=[pl.BlockSpec((PEER_STEP * PEER_SEL,), lambda i: (i,), memory_space=pltpu.SMEM),
                  pl.BlockSpec((PEER_STEP * PEER_SEL,), lambda i: (jnp.minimum(i + 1, n - 1),), memory_space=pltpu.SMEM),
                  pl.BlockSpec((PEER_STEP, D), lambda i: (i, 0)),
                  pl.BlockSpec((PEER_SEL, GATE_TILE), lambda i: (0, i // per_tile)),
                  pl.BlockSpec((PEER_STEP, D), lambda i: (i, 0)),
                  pl.BlockSpec(memory_space=pl.ANY)],
        out_specs=pl.BlockSpec((PEER_STEP, D), lambda i: (i, 0)),
        out_shape=jax.ShapeDtypeStruct((T, D), F32),
        scratch_shapes=[pltpu.VMEM(slot_shape, uv.dtype), pltpu.VMEM(slot_shape, uv.dtype),
                        pltpu.SemaphoreType.DMA((PEER_SLOTS,))],
        compiler_params=_cparams(("arbitrary",)),
    )(idx.reshape(-1), idx.reshape(-1), h, gate_t, x, uv)


def _ple_epilogue(acc, j, extra, out_ref):
    x_ref, p_ref, wp_ref = extra
    out_ref[...] = x_ref[...] + _sigmoid(acc) * _dot(p_ref[...], wp_ref[...])


FOX_SAMPLE_PAGES = 2


def _fox_sample_kernel(pt_ref, q_ref, ks_ref, vs_ref, lfs_ref, *rest, scale):
    cache_refs = rest[:3 * FOX_SAMPLE_PAGES]
    o_ref, m_sc, l_sc, acc_sc, carry_sc = rest[3 * FOX_SAMPLE_PAGES:]
    j = pl.program_id(1)
    NP = FOX_SAMPLE_PAGES
    kc_refs, vc_refs, lfc_refs = cache_refs[:NP], cache_refs[NP:2 * NP], cache_refs[2 * NP:]
    H, PS, HD = FOX_HEADS, kc_refs[0].shape[0], FOX_HD
    q = q_ref[0]

    @pl.when(j == 0)
    def _():
        m_sc[...] = jnp.sum(q * ks_ref[0], axis=-1, keepdims=True) * scale
        l_sc[...] = jnp.ones_like(l_sc)
        acc_sc[...] = vs_ref[0]
        carry_sc[...] = lfs_ref[0][:, 0:1]

    ones = jnp.ones((HD, LANES), BF16)
    pos3 = lax.broadcasted_iota(jnp.int32, (PS, H, LANES), 0)
    lane3 = lax.broadcasted_iota(jnp.int32, (PS, H, LANES), 2)
    diag = pos3 == lane3
    jj = lax.broadcasted_iota(jnp.int32, (PS, PS), 0)
    pp = lax.broadcasted_iota(jnp.int32, (PS, PS), 1)
    after = (jj > pp).astype(F32)
    qk = [_dot((kc[...] * q[None]).reshape(PS * H, HD).astype(BF16), ones).reshape(PS, H, LANES) for kc in kc_refs]
    lf_t = [lfc[0] for lfc in lfc_refs]
    carry = [carry_sc[...]]
    for a in range(NP):
        carry.append(carry[a] + jnp.sum(lf_t[a], axis=-1, keepdims=True))
    s_t = [jnp.sum(jnp.where(diag, qk[a], 0.0), axis=0) * scale + _dot(lf_t[a], after, HIGHEST) + carry[a]
           for a in range(NP)]
    m_old = m_sc[...]
    m_new = m_old
    for a in range(NP):
        m_new = jnp.maximum(m_new, jnp.max(s_t[a], axis=-1, keepdims=True))
    alpha = jnp.exp(m_old - m_new)
    p_t = [jnp.exp(s - m_new) for s in s_t]
    l_new = alpha * l_sc[...]
    for a in range(NP):
        l_new = l_new + jnp.sum(p_t[a], axis=-1, keepdims=True)
    l_sc[...] = l_new
    p3 = [_dot(jnp.where(diag, p[None], 0.0).reshape(PS * H, LANES).astype(BF16), ones).reshape(PS, H, LANES) for p in p_t]
    acc = alpha * acc_sc[...]
    for a in range(NP):
        acc = acc + jnp.sum(p3[a] * vc_refs[a][...], axis=0)
    acc_sc[...] = acc
    m_sc[...] = m_new
    carry_sc[...] = carry[NP]

    @pl.when(j == pl.num_programs(1) - 1)
    def _():
        o_ref[0] = (acc_sc[...] / l_sc[...]).astype(o_ref.dtype)


def fox_sample_attention(page_table, q, k_self, v_self, lf_self, cache_k, cache_v, cache_lf_t, layer):
    Bs, n_pages = page_table.shape
    PS = cache_k.shape[2]
    H, HD = FOX_HEADS, FOX_HD

    NP = FOX_SAMPLE_PAGES
    assert n_pages % NP == 0 and PS == LANES

    def kv_spec(a):
        return pl.BlockSpec((None, None, PS, H, HD),
                            lambda b, j, pt: (layer, pt[b, n_pages - 1 - (j * NP + a)], 0, 0, 0))

    def lf_spec(a):
        return pl.BlockSpec((1, H, PS), lambda b, j, pt: (pt[b, n_pages - 1 - (j * NP + a)], 0, 0))

    grid_spec = pltpu.PrefetchScalarGridSpec(
        num_scalar_prefetch=1,
        grid=(Bs, n_pages // NP),
        in_specs=[pl.BlockSpec((1, H, HD), lambda b, j, pt: (b, 0, 0)),
                  pl.BlockSpec((1, H, HD), lambda b, j, pt: (b, 0, 0)),
                  pl.BlockSpec((1, H, HD), lambda b, j, pt: (b, 0, 0)),
                  pl.BlockSpec((1, H, LANES), lambda b, j, pt: (b, 0, 0)),
                  *[kv_spec(a) for a in range(NP)], *[kv_spec(a) for a in range(NP)],
                  *[lf_spec(a) for a in range(NP)]],
        out_specs=pl.BlockSpec((1, H, HD), lambda b, j, pt: (b, 0, 0)),
        scratch_shapes=[pltpu.VMEM((H, 1), F32), pltpu.VMEM((H, 1), F32),
                        pltpu.VMEM((H, HD), F32), pltpu.VMEM((H, 1), F32)],
    )
    return pl.pallas_call(
        functools.partial(_fox_sample_kernel, scale=FOX_HD ** -0.5),
        grid_spec=grid_spec,
        out_shape=jax.ShapeDtypeStruct((Bs, H, HD), BF16),
        compiler_params=_cparams(("parallel", "arbitrary")),
    )(page_table, q, k_self, v_self, lf_self, *[cache_k] * NP, *[cache_v] * NP, *[cache_lf_t] * NP)


def _gdn_sample_kernel(x_ref, cs_ref, w_ref, sm_ref, z_ref, gn_ref, s_ref, o_ref, s_out_ref, conv_out_ref):
    x = x_ref[0]
    cs = cs_ref[0]
    w = w_ref[...]
    y = x * w[CONV_W - 1]
    for jx in range(CONV_W - 1):
        y = y + cs[jx] * w[jx]
    y = _silu(y)
    conv_out_ref[0, 0:CONV_W - 2] = cs[1:]
    conv_out_ref[0, CONV_W - 2] = x
    nh = GDN_HEADS
    normed = y * lax.rsqrt(jnp.sum(y * y, axis=-1, keepdims=True) + EPS)
    rowid = lax.broadcasted_iota(jnp.int32, normed.shape, 0)
    qk = jnp.where(rowid < nh, normed * (GDN_DK ** -0.5), normed)
    pad = jnp.zeros((LANES - 2 * nh, GDN_DK), F32)
    qk_t = jnp.concatenate([qk[:2 * nh], pad], axis=0).T
    sm = sm_ref[0]
    outs = []
    for h in range(nh):
        q_col = qk_t[:, h:h + 1]
        k_col = qk_t[:, nh + h:nh + h + 1]
        v_row = y[2 * nh + h:2 * nh + h + 1, :]
        g = sm[:, SM_G + h:SM_G + h + 1]
        beta = sm[:, SM_BETA + h:SM_BETA + h + 1]
        S = s_ref[0, h] * jnp.exp(g)
        kv = jnp.sum(k_col * S, axis=0, keepdims=True)
        S = S + k_col * ((v_row - kv) * beta)
        s_out_ref[0, h] = S
        outs.append(jnp.sum(q_col * S, axis=0, keepdims=True))
    o = jnp.concatenate(outs, axis=0)
    o_ref[0] = _gdn_out(o, z_ref[0], gn_ref[...]).astype(o_ref.dtype)


def gdn_sample(x24, conv_state, conv_w, sm, z, gn, state):
    Bs = x24.shape[0]
    nch = CONV_CH // LANES
    H = GDN_HEADS
    return pl.pallas_call(
        _gdn_sample_kernel,
        grid=(Bs,),
        in_specs=[pl.BlockSpec((1, nch, LANES), lambda b: (b, 0, 0)),
                  pl.BlockSpec((1, CONV_W - 1, nch, LANES), lambda b: (b, 0, 0, 0)),
                  pl.BlockSpec((CONV_W, nch, LANES), lambda b: (0, 0, 0)),
                  pl.BlockSpec((1, 1, LANES), lambda b: (b, 0, 0)),
                  pl.BlockSpec((1, H, GDN_DV), lambda b: (b, 0, 0)),
                  pl.BlockSpec((1, GDN_DV), lambda b: (0, 0)),
                  pl.BlockSpec((1, H, GDN_DK, GDN_DV), lambda b: (b, 0, 0, 0))],
        out_specs=[pl.BlockSpec((1, H, GDN_DV), lambda b: (b, 0, 0)),
                   pl.BlockSpec((1, H, GDN_DK, GDN_DV), lambda b: (b, 0, 0, 0)),
                   pl.BlockSpec((1, CONV_W - 1, nch, LANES), lambda b: (b, 0, 0, 0))],
        out_shape=[jax.ShapeDtypeStruct((Bs, H, GDN_DV), BF16),
                   jax.ShapeDtypeStruct((Bs, H, GDN_DK, GDN_DV), F32),
                   jax.ShapeDtypeStruct((Bs, CONV_W - 1, nch, LANES), F32)],
        compiler_params=_cparams(("parallel",)),
    )(x24, conv_state, conv_w, sm, z, gn, state)


PROJ_TN = 512


def _pack_layer(lw, D):
    w_in = lw['w_in']
    small = jnp.concatenate([w_in[:, OFF_FF:OFF_GQKV], w_in[:, OFF_GA:OFF_GB], w_in[:, OFF_GB:OFF_GZ]], axis=1)
    small = jnp.pad(small, ((0, 0), (0, PROJ_TN - small.shape[1])))
    w_proj = jnp.concatenate([w_in[:, OFF_FQ:OFF_FF], w_in[:, OFF_GQKV:OFF_GA], w_in[:, OFF_GZ:], small], axis=1).astype(BF16)
    n_main = P_GATE + 2 * D
    gains = jnp.concatenate([jnp.tile(lw['fox_q_norm'], FOX_HEADS), jnp.tile(lw['fox_k_norm'], FOX_HEADS),
                             jnp.zeros((n_main + PROJ_TN - 2 * FOX_W,), F32)])[None, :]
    zeros = jnp.zeros((LANES - SM_BETA,), F32)
    par = jnp.stack([jnp.concatenate([lw['fox_f_bias'], lw['gdn_dt_bias'], zeros]),
                     jnp.concatenate([jnp.zeros((SM_G,), F32), lw['gdn_a_log'], zeros])])
    par = jnp.pad(par, ((0, SUBLANES - 2), (0, 0)))
    return dict(
        w_proj=w_proj, gains=gains, par=par, n_main=n_main,
        norm_mix_g=lw['norm_mix_g'][None, :], conv_w=lw['gdn_conv_w'], gdn_norm_g=lw['gdn_norm_g'][None, :],
        w_up_fox=lw['w_up_fox'].astype(BF16), w_up_gdn=lw['w_up_gdn'].astype(BF16), w_out=lw['w_out'].astype(BF16),
        norm_ffn_g=lw['norm_ffn_g'][None, :], peer_w_q=lw['peer_w_q'].astype(BF16),
        sub_keys=lw['peer_sub_keys'].astype(BF16),
        uv=jnp.concatenate([lw['peer_u'].astype(BF16), lw['peer_v'].astype(BF16)], axis=1).reshape(-1, 2 * D // LANES, LANES),
        norm_ple_g=lw['norm_ple_g'][None, :], w_ple=lw['w_ple'].astype(BF16), w_ple_gate=lw['w_ple_gate'].astype(BF16),
    )


def _project(x, pk, tm):
    D = x.shape[1]
    tn = PROJ_TN
    n_qk = 2 * FOX_W // tn
    n_gate = 2 * D // tn
    n_plain = pk['n_main'] // tn - n_qk - n_gate
    epi = functools.partial(_proj_epilogue, tn=tn, n_qk=n_qk, n_plain=n_plain, n_gate=n_gate)
    return normed_linear(
        x, pk['norm_mix_g'], pk['w_proj'], tm=tm, tn=tn, epilogue=epi,
        extra=(pk['gains'], pk['par']),
        extra_specs=(pl.BlockSpec((1, tn), lambda i, j: (0, j)), pl.BlockSpec((SUBLANES, LANES), lambda i, j: (0, 0))))


def _channel_and_ple(x1, p, pk, tm):
    T, D = x1.shape
    Tp = -(-T // GATE_TILE) * GATE_TILE
    q, hn = normed_linear(x1, pk['norm_ffn_g'], pk['peer_w_q'], tm=tm, tn=512, epilogue=_plain_epilogue, out_dtype=BF16, emit_h=True)
    if Tp != T:
        q = jnp.pad(q, ((0, Tp - T), (0, 0)))
    idx_t, gate_t = peer_route(q, pk['sub_keys'], tb=512)
    idx = idx_t.reshape(PEER_SEL, Tp).T[:T]
    x2 = peer_experts(idx, gate_t.reshape(PEER_SEL, Tp), hn, x1, pk['uv'])
    tn = 512
    return normed_linear(
        x2, pk['norm_ple_g'], pk['w_ple_gate'], tm=tm, tn=tn, epilogue=_ple_epilogue,
        extra=(x2, p.astype(BF16), pk['w_ple']),
        extra_specs=(pl.BlockSpec((min(tm, T), tn), lambda i, j: (i, j)),
                     pl.BlockSpec((min(tm, T), p.shape[1]), lambda i, j: (i, 0)),
                     pl.BlockSpec((p.shape[1], tn), lambda i, j: (0, j))))


def _merge(x, o_fox, o_gdn, P, pk, tm):
    D = x.shape[1]
    m = merge_up(o_fox, o_gdn, pk['w_up_fox'], pk['w_up_gdn'], P, D, tm, 512)
    return linear_residual(m, pk['w_out'], x, tm, 512)


def _block_prompt(x3, p3, pk):
    B, L, D = x3.shape
    T = B * L
    x = x3.reshape(T, D)
    sm_block = pk['n_main'] // LANES
    P = _project(x, pk, tm=512)
    c_col = seq_cumsum(P, B, L, sm_block)
    c_row = c_col.reshape(B, L, LANES)[:, :, SM_LOGF:SM_LOGF + FOX_HEADS].transpose(0, 2, 1)
    o_fox = fox_prompt_attention(P, c_col, c_row, B, L)
    Y = gdn_prep(P, pk['conv_w'], B, L)
    o_gdn, s_new = gdn_chunked(Y, P, pk['gdn_norm_g'], B, L, sm_block)
    x1 = _merge(x, o_fox, o_gdn, P, pk, tm=512)
    y = _channel_and_ple(x1, p3.reshape(T, -1), pk, tm=512)
    k = P[:, P_K:P_K + FOX_W].reshape(B, L, FOX_HEADS, FOX_HD)
    v = P[:, P_V:P_V + FOX_W].reshape(B, L, FOX_HEADS, FOX_HD)
    logf = P[:, pk['n_main'] + SM_LOGF:pk['n_main'] + SM_LOGF + FOX_HEADS].reshape(B, L, FOX_HEADS)
    conv_new = P[:, P_GQKV:P_GQKV + CONV_CH].reshape(B, L, CONV_CH)[:, L - (CONV_W - 1):]
    return y.reshape(B, L, D), k, v, logf, s_new, conv_new


def _block_sample(x3, p3, cache_k, cache_v, cache_logf, s0, conv0, page_table, layer, pk):
    Bs, Ls, D = x3.shape
    assert Ls == 1
    x = x3.reshape(Bs, D)
    n_main = pk['n_main']
    nch = CONV_CH // LANES
    P = _project(x, pk, tm=Bs)
    q = P[:, P_Q:P_Q + FOX_W].reshape(Bs, FOX_HEADS, FOX_HD)
    k = P[:, P_K:P_K + FOX_W].reshape(Bs, FOX_HEADS, FOX_HD)
    v = P[:, P_V:P_V + FOX_W].reshape(Bs, FOX_HEADS, FOX_HD)
    sm = P[:, n_main:n_main + LANES]
    logf = sm[:, SM_LOGF:SM_LOGF + FOX_HEADS]
    lf_b = jnp.broadcast_to(logf[:, :, None], (Bs, FOX_HEADS, LANES))
    cache_lf_t = cache_logf[layer].transpose(0, 2, 1)
    o_fox = fox_sample_attention(page_table, q, k, v, lf_b, cache_k, cache_v, cache_lf_t, layer)
    x24 = P[:, P_GQKV:P_GQKV + CONV_CH].reshape(Bs, nch, LANES)
    z = P[:, P_Z:P_Z + GDN_VW].reshape(Bs, GDN_HEADS, GDN_DV)
    o_gdn, s_new, conv_new = gdn_sample(
        x24, conv0.reshape(Bs, CONV_W - 1, nch, LANES), pk['conv_w'].reshape(CONV_W, nch, LANES),
        sm.reshape(Bs, 1, LANES), z, pk['gdn_norm_g'], s0)
    x1 = _merge(x, o_fox.reshape(Bs, FOX_W), o_gdn.reshape(Bs, GDN_VW), P, pk, tm=Bs)
    y = _channel_and_ple(x1, p3.reshape(Bs, -1), pk, tm=Bs)
    return (y.reshape(Bs, 1, D), k.reshape(Bs, 1, FOX_HEADS, FOX_HD), v.reshape(Bs, 1, FOX_HEADS, FOX_HD),
            logf.reshape(Bs, 1, FOX_HEADS), s_new, conv_new.reshape(Bs, CONV_W - 1, CONV_CH))


def kernel(x_prompt, x_sample, p_prompt, p_sample, cache_fox_k, cache_fox_v, cache_fox_logf, state_gdn, state_conv, page_table, norm_mix_g, w_in, fox_f_bias, fox_q_norm, fox_k_norm, gdn_conv_w, gdn_a_log, gdn_dt_bias, gdn_norm_g, w_up_fox, w_up_gdn, w_out, norm_ffn_g, peer_w_q, peer_sub_keys, peer_u, peer_v, norm_ple_g, w_ple, w_ple_gate):
    depth = w_in.shape[0]
    D = x_prompt.shape[-1]
    xp, xs = x_prompt, x_sample
    outs = [[] for _ in range(10)]
    for i in range(depth):
        lw = {
            'norm_mix_g': norm_mix_g[i], 'w_in': w_in[i], 'fox_f_bias': fox_f_bias[i],
            'fox_q_norm': fox_q_norm[i], 'fox_k_norm': fox_k_norm[i], 'gdn_conv_w': gdn_conv_w[i],
            'gdn_a_log': gdn_a_log[i], 'gdn_dt_bias': gdn_dt_bias[i], 'gdn_norm_g': gdn_norm_g[i],
            'w_up_fox': w_up_fox[i], 'w_up_gdn': w_up_gdn[i], 'w_out': w_out[i], 'norm_ffn_g': norm_ffn_g[i],
            'peer_w_q': peer_w_q[i], 'peer_sub_keys': peer_sub_keys[i], 'peer_u': peer_u[i], 'peer_v': peer_v[i],
            'norm_ple_g': norm_ple_g[i], 'w_ple': w_ple[i], 'w_ple_gate': w_ple_gate[i],
        }
        pk = _pack_layer(lw, D)
        xp, kp, vp, lfp, sp, cp = _block_prompt(xp, p_prompt[i], pk)
        xs, ks_, vs_, lfs, ss, cs = _block_sample(xs, p_sample[i], cache_fox_k, cache_fox_v, cache_fox_logf,
                                                  state_gdn[i], state_conv[i], page_table, i, pk)
        for lst, val in zip(outs, (kp, vp, lfp, sp, cp, ks_, vs_, lfs, ss, cs)):
            lst.append(val)
    return (xp, xs) + tuple(jnp.stack(o) for o in outs)
```

```python
import functools
import math

import jax
import jax.numpy as jnp
from jax import lax
from jax.experimental import pallas as pl
from jax.experimental.pallas import tpu as pltpu

F32 = jnp.float32
BF16 = jnp.bfloat16
HIGHEST = lax.Precision.HIGHEST
EPS = 1e-6
NEG_BIG = -1e30

LANES = 128
SUBLANES = 8
MXU_DEPTH = 256
DMA_THREADS = 2
VMEM_LIMIT = 56 * 1024 * 1024

FOX_HEADS = 8
FOX_HD = 128
FOX_W = FOX_HEADS * FOX_HD
GDN_HEADS = 8
GDN_DK = 128
GDN_DV = 128
GDN_KW = GDN_HEADS * GDN_DK
GDN_VW = GDN_HEADS * GDN_DV
CONV_W = 4
CONV_CH = 2 * GDN_KW + GDN_VW
GDN_CHUNK = 64
PEER_HEADS = 8
N_KEYS = 128
PEER_DK = 128
PEER_TOPK = 16
PEER_SEL = PEER_HEADS * PEER_TOPK

OFF_FQ = 0
OFF_FK = OFF_FQ + FOX_W
OFF_FV = OFF_FK + FOX_W
OFF_FF = OFF_FV + FOX_W
OFF_GQKV = OFF_FF + FOX_HEADS
OFF_GA = OFF_GQKV + CONV_CH
OFF_GB = OFF_GA + GDN_HEADS
OFF_GZ = OFF_GB + GDN_HEADS
OFF_GATE = OFF_GZ + GDN_VW

P_Q = 0
P_K = P_Q + FOX_W
P_V = P_K + FOX_W
P_GQKV = P_V + FOX_W
P_Z = P_GQKV + CONV_CH
P_GATE = P_Z + GDN_VW
SM_LOGF = 0
SM_G = SM_LOGF + FOX_HEADS
SM_BETA = SM_G + GDN_HEADS


def _cparams(sem):
    return pltpu.CompilerParams(dimension_semantics=sem, vmem_limit_bytes=VMEM_LIMIT)


def _sigmoid(x):
    return 1.0 / (1.0 + jnp.exp(-x))


def _silu(x):
    return x * _sigmoid(x)


def _dot(a, b, precision=None):
    return jnp.dot(a, b, preferred_element_type=F32, precision=precision)


def _dot_nt(a, b, precision=None):
    return lax.dot_general(a, b, (((1,), (1,)), ((), ())), preferred_element_type=F32, precision=precision)


def _dot_tn(a, b, precision=None):
    return lax.dot_general(a, b, (((0,), (0,)), ((), ())), preferred_element_type=F32, precision=precision)


def _split(x):
    hi = x.astype(BF16)
    return hi, (x - hi.astype(F32)).astype(BF16)


def _mm3(dot, a, b):
    return dot(a[0], b[0]) + (dot(a[0], b[1]) + dot(a[1], b[0]))


def _normed_linear_kernel(*refs, n_extra, emit_h, epilogue):
    x_ref, g_ref, w_ref = refs[:3]
    extra = refs[3:3 + n_extra]
    out_ref = refs[3 + n_extra]
    h_out = refs[4 + n_extra] if emit_h else None
    h_sc = refs[-1]
    j = pl.program_id(1)

    @pl.when(j == 0)
    def _():
        x = x_ref[...]
        y = x * lax.rsqrt(jnp.mean(x * x, axis=-1, keepdims=True) + EPS) * g_ref[...]
        h_sc[...] = y.astype(BF16)
        if emit_h:
            h_out[...] = y.astype(h_out.dtype)

    acc = _dot(h_sc[...], w_ref[...])
    epilogue(acc, j, extra, out_ref)


def normed_linear(x, g, w, *, tm, tn, epilogue, extra=(), extra_specs=(), out_dtype=F32, emit_h=False):
    T, D = x.shape
    N = w.shape[1]
    tm = min(tm, T)
    tn = min(tn, N)
    assert T % tm == 0 and N % tn == 0
    out_shape = [jax.ShapeDtypeStruct((T, N), out_dtype)]
    out_specs = [pl.BlockSpec((tm, tn), lambda i, j: (i, j))]
    if emit_h:
        out_shape.append(jax.ShapeDtypeStruct((T, D), F32))
        out_specs.append(pl.BlockSpec((tm, D), lambda i, j: (i, 0)))
    res = pl.pallas_call(
        functools.partial(_normed_linear_kernel, n_extra=len(extra), emit_h=emit_h, epilogue=epilogue),
        grid=(T // tm, N // tn),
        in_specs=[pl.BlockSpec((tm, D), lambda i, j: (i, 0)),
                  pl.BlockSpec((1, D), lambda i, j: (0, 0)),
                  pl.BlockSpec((D, tn), lambda i, j: (0, j)),
                  *extra_specs],
        out_specs=out_specs,
        out_shape=out_shape,
        scratch_shapes=[pltpu.VMEM((tm, D), BF16)],
        compiler_params=_cparams(("parallel", "arbitrary")),
    )(x, g, w, *extra)
    return res if emit_h else res[0]


def _plain_epilogue(acc, j, extra, out_ref):
    out_ref[...] = acc.astype(out_ref.dtype)


def _proj_epilogue(acc, j, extra, out_ref, *, tn, n_qk, n_plain, n_gate):
    gain_ref, par_ref = extra

    @pl.when(j < n_qk)
    def _():
        for c in range(tn // LANES):
            blk = acc[:, c * LANES:(c + 1) * LANES]
            ms = jnp.mean(blk * blk, axis=-1, keepdims=True)
            out_ref[:, c * LANES:(c + 1) * LANES] = blk * lax.rsqrt(ms + EPS) * gain_ref[:, c * LANES:(c + 1) * LANES]

    @pl.when(jnp.logical_and(j >= n_qk, j < n_qk + n_plain))
    def _():
        out_ref[...] = acc

    @pl.when(jnp.logical_and(j >= n_qk + n_plain, j < n_qk + n_plain + n_gate))
    def _():
        out_ref[...] = _sigmoid(acc)

    @pl.when(j == n_qk + n_plain + n_gate)
    def _():
        v = acc[:, :LANES] + par_ref[0:1, :]
        lane = lax.broadcasted_iota(jnp.int32, v.shape, 1)
        tail = jnp.log1p(jnp.exp(-jnp.abs(v)))
        logf = -(jnp.maximum(-v, 0.0) + tail)
        g = -jnp.exp(par_ref[1:2, :]) * (jnp.maximum(v, 0.0) + tail)
        beta = _sigmoid(v)
        res = jnp.where(lane < SM_G, logf, jnp.where(lane < SM_BETA, g, beta))
        out_ref[:, :LANES] = res
        if tn > LANES:
            out_ref[:, LANES:] = jnp.zeros((acc.shape[0], tn - LANES), F32)


def _cumsum_kernel(sm_ref, out_ref, carry_sc, *, tr):
    @pl.when(pl.program_id(1) == 0)
    def _():
        carry_sc[...] = jnp.zeros_like(carry_sc)

    r = lax.broadcasted_iota(jnp.int32, (tr, tr), 0)
    c = lax.broadcasted_iota(jnp.int32, (tr, tr), 1)
    tri = (c <= r).astype(F32)
    cs = _dot(tri, sm_ref[...], HIGHEST) + carry_sc[...]
    out_ref[...] = cs
    carry_sc[...] = cs[tr - 1:tr, :]


def seq_cumsum(P, B, L, col_block):
    tr = min(256, L)
    nb = L // tr
    return pl.pallas_call(
        functools.partial(_cumsum_kernel, tr=tr),
        grid=(B, nb),
        in_specs=[pl.BlockSpec((tr, LANES), lambda b, i: (b * nb + i, col_block))],
        out_specs=pl.BlockSpec((tr, LANES), lambda b, i: (b * nb + i, 0)),
        out_shape=jax.ShapeDtypeStruct((B * L, LANES), F32),
        scratch_shapes=[pltpu.VMEM((1, LANES), F32)],
        compiler_params=_cparams(("parallel", "arbitrary")),
    )(P)


def _fox_prompt_kernel(q_ref, k_ref, v_ref, cq_ref, ck_ref, o_ref, m_sc, l_sc, acc_sc, *, tq, tk, scale):
    qi = pl.program_id(1)
    kj = pl.program_id(2)

    @pl.when(kj == 0)
    def _():
        m_sc[...] = jnp.full(m_sc.shape, NEG_BIG, F32)
        l_sc[...] = jnp.zeros_like(l_sc)
        acc_sc[...] = jnp.zeros_like(acc_sc)

    @pl.when(kj * tk <= qi * tq + tq - 1)
    def _():
        q_pos = qi * tq + lax.broadcasted_iota(jnp.int32, (tq, tk), 0)
        k_pos = kj * tk + lax.broadcasted_iota(jnp.int32, (tq, tk), 1)
        keep = k_pos <= q_pos
        cq = cq_ref[...]
        ck = ck_ref[0]
        for h in range(FOX_HEADS):
            sl = slice(h * FOX_HD, (h + 1) * FOX_HD)
            s = _dot_nt(q_ref[:, sl].astype(BF16), k_ref[:, sl].astype(BF16)) * scale
            s = s + (cq[:, h:h + 1] - ck[h:h + 1, :])
            s = jnp.where(keep, s, NEG_BIG)
            m_old = m_sc[h]
            m_new = jnp.maximum(m_old, jnp.max(s, axis=-1, keepdims=True))
            alpha = jnp.exp(m_old - m_new)
            p = jnp.exp(s - m_new)
            l_sc[h] = alpha * l_sc[h] + jnp.sum(p, axis=-1, keepdims=True)
            acc_sc[:, sl] = alpha * acc_sc[:, sl] + _dot(p.astype(BF16), v_ref[:, sl].astype(BF16))
            m_sc[h] = m_new

    @pl.when(kj == pl.num_programs(2) - 1)
    def _():
        for h in range(FOX_HEADS):
            sl = slice(h * FOX_HD, (h + 1) * FOX_HD)
            o_ref[:, sl] = (acc_sc[:, sl] / l_sc[h]).astype(o_ref.dtype)


def fox_prompt_attention(P, c_col, c_row, B, L):
    tq = min(256, L)
    tk = min(512, L)
    nq, nk = L // tq, L // tk

    def kv_block(b, qi, kj):
        last = (qi * tq + tq - 1) // tk
        return b * nk + jnp.minimum(kj, last)

    return pl.pallas_call(
        functools.partial(_fox_prompt_kernel, tq=tq, tk=tk, scale=FOX_HD ** -0.5),
        grid=(B, nq, nk),
        in_specs=[pl.BlockSpec((tq, FOX_W), lambda b, qi, kj: (b * nq + qi, P_Q // FOX_W)),
                  pl.BlockSpec((tk, FOX_W), lambda b, qi, kj: (kv_block(b, qi, kj), P_K // FOX_W)),
                  pl.BlockSpec((tk, FOX_W), lambda b, qi, kj: (kv_block(b, qi, kj), P_V // FOX_W)),
                  pl.BlockSpec((tq, LANES), lambda b, qi, kj: (b * nq + qi, 0)),
                  pl.BlockSpec((1, FOX_HEADS, tk), lambda b, qi, kj: (b, 0, jnp.minimum(kj, (qi * tq + tq - 1) // tk)))],
        out_specs=pl.BlockSpec((tq, FOX_W), lambda b, qi, kj: (b * nq + qi, 0)),
        out_shape=jax.ShapeDtypeStruct((B * L, FOX_W), BF16),
        scratch_shapes=[pltpu.VMEM((FOX_HEADS, tq, 1), F32),
                        pltpu.VMEM((FOX_HEADS, tq, 1), F32),
                        pltpu.VMEM((tq, FOX_W), F32)],
        compiler_params=_cparams(("parallel", "parallel", "arbitrary")),
    )(P, P, P, c_col, c_row)


def _gdn_prep_kernel(x_ref, w_ref, o_ref):
    c = pl.program_id(1)
    x = x_ref[...]
    w = w_ref[...]
    row = lax.broadcasted_iota(jnp.int32, x.shape, 0)
    y = x * w[CONV_W - 1:CONV_W, :]
    for s in range(1, CONV_W):
        shifted = jnp.where(row >= s, pltpu.roll(x, s, 0), 0.0)
        y = y + shifted * w[CONV_W - 1 - s:CONV_W - s, :]
    y = _silu(y)
    normed = y * lax.rsqrt(jnp.sum(y * y, axis=-1, keepdims=True) + EPS)
    q_scale = jnp.where(c < GDN_HEADS, GDN_DK ** -0.5, 1.0)
    o_ref[...] = jnp.where(c < 2 * GDN_HEADS, normed * q_scale, y)


def gdn_prep(P, conv_w, B, L):
    nch = CONV_CH // LANES
    return pl.pallas_call(
        _gdn_prep_kernel,
        grid=(B, nch),
        in_specs=[pl.BlockSpec((L, LANES), lambda b, c: (b, P_GQKV // LANES + c)),
                  pl.BlockSpec((CONV_W, LANES), lambda b, c: (0, c))],
        out_specs=pl.BlockSpec((L, LANES), lambda b, c: (b, c)),
        out_shape=jax.ShapeDtypeStruct((B * L, CONV_CH), F32),
        compiler_params=_cparams(("parallel", "parallel")),
    )(P, conv_w)


def _gdn_out(o, z, gn):
    y = o * lax.rsqrt(jnp.mean(o * o, axis=-1, keepdims=True) + EPS) * gn
    return y * _silu(z)


GDN_LOCAL_CHUNKS = 2


def _gdn_local_kernel(y_ref, sm_ref, u_ref, w_ref, qg_ref, kd_ref, attn_ref, eg_ref):
    C = GDN_CHUNK
    assert 2 * C == LANES
    r = lax.broadcasted_iota(jnp.int32, (C, LANES), 0)
    c = lax.broadcasted_iota(jnp.int32, (C, LANES), 1)
    incl = c <= r
    strict = c < r
    r_sq = lax.broadcasted_iota(jnp.int32, (C, C), 0)
    c_sq = lax.broadcasted_iota(jnp.int32, (C, C), 1)
    eye = (c_sq == r_sq).astype(F32)
    tril = (c_sq <= r_sq).astype(F32)
    zpad = jnp.zeros((LANES - C, LANES), F32)
    pw, t_inv, tail = [], [], []
    for ch in range(GDN_LOCAL_CHUNKS):
        rows = slice(ch * C, (ch + 1) * C)
        sm = sm_ref[rows, :]
        gcum_all = _dot(tril, sm, HIGHEST)
        gcum_pad = jnp.concatenate([gcum_all, zpad], axis=0)
        eg_ref[rows, :] = jnp.exp(jnp.broadcast_to(gcum_all[C - 1:C, :], (C, LANES)))
        for h in range(GDN_HEADS):
            hs = slice(h * GDN_DK, (h + 1) * GDN_DK)
            q = y_ref[rows, h * GDN_DK:(h + 1) * GDN_DK]
            k = y_ref[rows, GDN_KW + h * GDN_DK:GDN_KW + (h + 1) * GDN_DK]
            v = y_ref[rows, 2 * GDN_KW + h * GDN_DV:2 * GDN_KW + (h + 1) * GDN_DV]
            gc = gcum_all[:, SM_G + h:SM_G + h + 1]
            gr = _dot_nt((c == SM_G + h).astype(F32), gcum_pad, HIGHEST)
            beta = sm[:, SM_BETA + h:SM_BETA + h + 1]
            g_last = gc[C - 1:C, :]
            decay = jnp.where(incl, jnp.exp(jnp.where(incl, gc - gr, 0.0)), 0.0)
            kb = k * beta
            k2 = _split(jnp.concatenate([k, zpad], axis=0))
            a = jnp.where(strict, _mm3(_dot_nt, _split(kb), k2) * decay, 0.0)[:, :C]
            qg_ref[rows, hs] = q * jnp.exp(gc)
            kd_ref[rows, hs] = k * jnp.exp(g_last - gc)
            attn_ref[rows, h * LANES:(h + 1) * LANES] = _mm3(_dot_nt, _split(q), k2) * decay
            pw.append(-a)
            t_inv.append(eye - a)
            tail.append((rows, hs, _split(v * beta), _split(kb * jnp.exp(gc))))
    for _ in range(int(math.log2(C)) - 1):
        for n in range(len(pw)):
            pw2 = _split(pw[n])
            pw[n] = _mm3(_dot, pw2, pw2)
        for n in range(len(pw)):
            t_inv[n] = t_inv[n] + _mm3(_dot, _split(t_inv[n]), _split(pw[n]))
    for n, (rows, hs, vb2, kg2) in enumerate(tail):
        t2 = _split(t_inv[n])
        u_ref[rows, hs] = _mm3(_dot, t2, vb2)
        w_ref[rows, hs] = _mm3(_dot, t2, kg2)


def _gdn_scan_kernel(u_ref, w_ref, qg_ref, kd_ref, attn_ref, eg_ref, z_ref, gn_ref, o_ref, s_out_ref, s_sc):
    ci = pl.program_id(1)
    C = GDN_CHUNK

    @pl.when(ci == 0)
    def _():
        s_sc[...] = jnp.zeros_like(s_sc)

    gn = gn_ref[...]
    eg_row = eg_ref[...][0:1, :]
    heads = range(GDN_HEADS)
    hs = [slice(h * GDN_DV, (h + 1) * GDN_DV) for h in heads]
    S = [s_sc[h] for h in heads]
    S2 = [_split(S[h]) for h in heads]
    v2 = [_split(u_ref[:, hs[h]] - _mm3(_dot, _split(w_ref[:, hs[h]]), S2[h])) for h in heads]
    o_state = [_mm3(_dot, _split(qg_ref[:, hs[h]]), S2[h]) for h in heads]
    for h in heads:
        o = o_state[h] + _mm3(_dot, _split(attn_ref[:, h * LANES:h * LANES + C]), v2[h])
        o_ref[:, hs[h]] = _gdn_out(o, z_ref[:, hs[h]], gn).astype(o_ref.dtype)
    for h in heads:
        s_sc[h] = S[h] * eg_row[:, SM_G + h:SM_G + h + 1] + _mm3(_dot_tn, _split(kd_ref[:, hs[h]]), v2[h])

    @pl.when(ci == pl.num_programs(1) - 1)
    def _():
        s_out_ref[0] = s_sc[...]


def gdn_chunked(Y, P, gn, B, L, sm_block):
    C = GDN_CHUNK
    n = L // C
    T = B * L
    rows = C * GDN_LOCAL_CHUNKS
    assert T % rows == 0
    wide = jax.ShapeDtypeStruct((T, GDN_VW), F32)
    u, w, qg, kd, attn, eg = pl.pallas_call(
        _gdn_local_kernel,
        grid=(T // rows,),
        in_specs=[pl.BlockSpec((rows, CONV_CH), lambda i: (i, 0)),
                  pl.BlockSpec((rows, LANES), lambda i: (i, sm_block))],
        out_specs=[pl.BlockSpec((rows, GDN_VW), lambda i: (i, 0))] * 4
                  + [pl.BlockSpec((rows, GDN_HEADS * LANES), lambda i: (i, 0)), pl.BlockSpec((rows, LANES), lambda i: (i, 0))],
        out_shape=[wide] * 4 + [jax.ShapeDtypeStruct((T, GDN_HEADS * LANES), F32), jax.ShapeDtypeStruct((T, LANES), F32)],
        compiler_params=_cparams(("parallel",)),
    )(Y, P)
    chunk = lambda width: pl.BlockSpec((C, width), lambda b, i: (b * n + i, 0))
    return pl.pallas_call(
        _gdn_scan_kernel,
        grid=(B, n),
        in_specs=[chunk(GDN_VW), chunk(GDN_VW), chunk(GDN_VW), chunk(GDN_VW), chunk(GDN_HEADS * LANES), chunk(LANES),
                  pl.BlockSpec((C, GDN_VW), lambda b, i: (b * n + i, P_Z // GDN_VW)),
                  pl.BlockSpec((1, GDN_DV), lambda b, i: (0, 0))],
        out_specs=[pl.BlockSpec((C, GDN_VW), lambda b, i: (b * n + i, 0)),
                   pl.BlockSpec((1, GDN_HEADS, GDN_DK, GDN_DV), lambda b, i: (b, 0, 0, 0))],
        out_shape=[jax.ShapeDtypeStruct((T, GDN_VW), BF16),
                   jax.ShapeDtypeStruct((B, GDN_HEADS, GDN_DK, GDN_DV), F32)],
        scratch_shapes=[pltpu.VMEM((GDN_HEADS, GDN_DK, GDN_DV), F32)],
        compiler_params=_cparams(("parallel", "arbitrary")),
    )(u, w, qg, kd, attn, eg, P, gn)


def _merge_up_kernel(of_ref, og_ref, wf_ref, wg_ref, ga_ref, gb_ref, m_ref):
    m = ga_ref[...] * _dot(of_ref[...], wf_ref[...]) + gb_ref[...] * _dot(og_ref[...], wg_ref[...])
    m_ref[...] = m.astype(m_ref.dtype)


def merge_up(o_fox, o_gdn, w_up_fox, w_up_gdn, P, D, tm, tn):
    T = o_fox.shape[0]
    tm = min(tm, T)
    gate_block = P_GATE // tn
    return pl.pallas_call(
        _merge_up_kernel,
        grid=(T // tm, D // tn),
        in_specs=[pl.BlockSpec((tm, FOX_W), lambda i, j: (i, 0)),
                  pl.BlockSpec((tm, GDN_VW), lambda i, j: (i, 0)),
                  pl.BlockSpec((FOX_W, tn), lambda i, j: (0, j)),
                  pl.BlockSpec((GDN_VW, tn), lambda i, j: (0, j)),
                  pl.BlockSpec((tm, tn), lambda i, j: (i, gate_block + j)),
                  pl.BlockSpec((tm, tn), lambda i, j: (i, gate_block + D // tn + j))],
        out_specs=pl.BlockSpec((tm, tn), lambda i, j: (i, j)),
        out_shape=jax.ShapeDtypeStruct((T, D), BF16),
        compiler_params=_cparams(("parallel", "parallel")),
    )(o_fox, o_gdn, w_up_fox, w_up_gdn, P, P)


def _linear_residual_kernel(a_ref, w_ref, x_ref, o_ref):
    o_ref[...] = x_ref[...] + _dot(a_ref[...], w_ref[...])


def linear_residual(a, w, x, tm, tn):
    T, K = a.shape
    N = w.shape[1]
    tm = min(tm, T)
    return pl.pallas_call(
        _linear_residual_kernel,
        grid=(T // tm, N // tn),
        in_specs=[pl.BlockSpec((tm, K), lambda i, j: (i, 0)),
                  pl.BlockSpec((K, tn), lambda i, j: (0, j)),
                  pl.BlockSpec((tm, tn), lambda i, j: (i, j))],
        out_specs=pl.BlockSpec((tm, tn), lambda i, j: (i, j)),
        out_shape=jax.ShapeDtypeStruct((T, N), F32),
        compiler_params=_cparams(("parallel", "parallel")),
    )(a, w, x)


def _top16(s, ids, val_sc, idx_sc):
    for it in range(PEER_TOPK):
        m = jnp.max(s, axis=0, keepdims=True)
        idx = jnp.min(jnp.where(s == m, ids, NO_ID), axis=0, keepdims=True)
        val_sc[it:it + 1, :] = m
        idx_sc[it:it + 1, :] = idx
        s = jnp.where(ids == idx, -jnp.inf, s)


NO_ID = 1e9
PAIR_COUNTS = tuple(PEER_TOPK // (i + 1) for i in range(PEER_TOPK))
PAIR_STARTS = tuple(sum(PAIR_COUNTS[:i]) for i in range(PEER_TOPK))
N_PAIRS = sum(PAIR_COUNTS)
N_PAIR_ROWS = -(-N_PAIRS // SUBLANES) * SUBLANES


def _peer_route_kernel(q_ref, keys_ref, idx_ref, gate_ref, va_sc, ia_sc, vb_sc, ib_sc, vc_sc, ic_sc, cand_sc):
    K = PEER_TOPK
    tb = q_ref.shape[0]
    key_ids = lax.broadcasted_iota(jnp.int32, (N_KEYS, tb), 0).astype(F32)
    sa = _dot_nt(keys_ref[0, 0], q_ref[:, :PEER_DK])
    _top16(sa, key_ids, va_sc, ia_sc)
    sb = _dot_nt(keys_ref[0, 1], q_ref[:, PEER_DK:])
    _top16(sb, key_ids, vb_sc, ib_sc)
    cand_sc[N_PAIR_ROWS - SUBLANES:, :] = jnp.full((SUBLANES, tb), -jnp.inf, F32)
    row = lax.broadcasted_iota(jnp.int32, (N_PAIR_ROWS, tb), 0)
    shift = jnp.zeros((N_PAIR_ROWS, tb), jnp.int32)
    for i in range(K):
        cand_sc[PAIR_STARTS[i]:PAIR_STARTS[i] + PAIR_COUNTS[i], :] = va_sc[i:i + 1, :] + vb_sc[0:PAIR_COUNTS[i], :]
        if i:
            shift = jnp.where(row >= PAIR_STARTS[i], i * K - PAIR_STARTS[i], shift)
    pair_ids = jnp.where(row < N_PAIRS, (row + shift).astype(F32), NO_ID)
    _top16(cand_sc[...], pair_ids, vc_sc, ic_sc)
    cv, ci = vc_sc[...], ic_sc[...]
    hi = jnp.floor(ci * (1.0 / K))
    lo = ci - hi * K
    ia_all, ib_all = ia_sc[...], ib_sc[...]
    ia = jnp.zeros_like(ci)
    ib = jnp.zeros_like(ci)
    for i in range(K):
        ia = jnp.where(hi == i, ia_all[i:i + 1, :], ia)
        ib = jnp.where(lo == i, ib_all[i:i + 1, :], ib)
    idx_ref[0] = (ia * N_KEYS + ib).astype(jnp.int32)
    e = jnp.exp(cv - cv[0:1, :])
    gate_ref[0] = e / jnp.sum(e, axis=0, keepdims=True)


def peer_route(q, sub_keys, tb):
    T = q.shape[0]
    tb = min(tb, T)
    K = PEER_TOPK
    return pl.pallas_call(
        _peer_route_kernel,
        grid=(T // tb, PEER_HEADS),
        in_specs=[pl.BlockSpec((tb, 2 * PEER_DK), lambda i, h: (i, h)),
                  pl.BlockSpec((1, 2, N_KEYS, PEER_DK), lambda i, h: (h, 0, 0, 0))],
        out_specs=[pl.BlockSpec((1, K, tb), lambda i, h: (h, 0, i)),
                   pl.BlockSpec((1, K, tb), lambda i, h: (h, 0, i))],
        out_shape=[jax.ShapeDtypeStruct((PEER_HEADS, K, T), jnp.int32),
                   jax.ShapeDtypeStruct((PEER_HEADS, K, T), F32)],
        scratch_shapes=[pltpu.VMEM((K, tb), F32) for _ in range(6)] + [pltpu.VMEM((N_PAIR_ROWS, tb), F32)],
        compiler_params=_cparams(("parallel", "parallel")),
    )(q, sub_keys)


PEER_TB = 8
PEER_SLOTS = 4
PEER_LOOKAHEAD = 2
PEER_STEP = PEER_TB * PEER_SLOTS
PEER_ISSUE_UNROLL = 8
PEER_INTERLEAVE = 4
GATE_TILE = 128


def _peer_expert_kernel(idx_ref, idx_next_ref, h_ref, gate_ref, x_ref, uv_hbm, o_ref, *scratch, D):
    i = pl.program_id(0)
    n = pl.num_programs(0)
    S = 2 * D // LANES
    SU = D // LANES
    bufs, sem = scratch[:PEER_SLOTS], scratch[PEER_SLOTS]
    slot_experts = PEER_TB * PEER_SEL

    def start_fetch(idx, slot):
        for t in range(PEER_TB):
            def body(kb, carry, t=t):
                for kk in range(PEER_ISSUE_UNROLL):
                    k = kb * PEER_ISSUE_UNROLL + kk
                    pltpu.make_async_copy(uv_hbm.at[idx[(slot * PEER_TB + t) * PEER_SEL + k]],
                                          bufs[slot].at[t * PEER_SEL + k], sem.at[slot]).start()
                return carry
            lax.fori_loop(0, PEER_SEL // PEER_ISSUE_UNROLL, body, 0)

    def wait_slot(slot):
        pltpu.make_async_copy(uv_hbm.at[pl.ds(0, slot_experts)], bufs[slot], sem.at[slot]).wait()

    tok = lax.broadcasted_iota(jnp.int32, (GATE_TILE, PEER_STEP), 0)
    col = lax.broadcasted_iota(jnp.int32, (GATE_TILE, PEER_STEP), 1)
    first = (i % (GATE_TILE // PEER_STEP)) * PEER_STEP
    onehot = (tok == first + col).astype(F32)
    gates = _dot(gate_ref[...], onehot, HIGHEST)

    def iota2(shape, axis):
        return lax.broadcasted_iota(jnp.int32, shape, axis)

    grp = MXU_DEPTH // SU
    row_group_sum = (iota2((grp, MXU_DEPTH), 1) // SU == iota2((grp, MXU_DEPTH), 0)).astype(BF16)
    lane0 = (iota2((SU, LANES), 1) == 0).astype(BF16)
    expand = (iota2((PEER_SEL, PEER_SEL * SU), 1) // SU == iota2((PEER_SEL, PEER_SEL * SU), 0)).astype(BF16)
    own_row = iota2((SU, PEER_SEL * SU), 1) % SU == iota2((SU, PEER_SEL * SU), 0)

    def compute(slot, ahead, ahead_idx):
        buf = bufs[slot]
        for t0 in range(0, PEER_TB, PEER_INTERLEAVE):
            toks = range(t0, t0 + PEER_INTERLEAVE)
            for t in toks:
                for k in range(PEER_SEL):
                    pltpu.make_async_copy(uv_hbm.at[ahead_idx[(ahead * PEER_TB + t) * PEER_SEL + k]],
                                          bufs[ahead].at[t * PEER_SEL + k], sem.at[ahead]).start(
                                              priority=k % DMA_THREADS)
            lane_part = []
            for t in toks:
                tt = slot * PEER_TB + t
                h = h_ref[tt:tt + 1, :]
                h2 = jnp.concatenate([h[:, c * LANES:(c + 1) * LANES] for c in range(SU)], axis=0).astype(BF16)
                prod = (buf[t * PEER_SEL:(t + 1) * PEER_SEL, :SU, :] * h2[None]).reshape(PEER_SEL * SU, LANES)
                lane_part.append(jnp.concatenate(
                    [_dot(row_group_sum, prod[g * MXU_DEPTH:(g + 1) * MXU_DEPTH, :]) for g in range(PEER_SEL // grp)],
                    axis=0))
            a_lanes = []
            for t, lp in zip(toks, lane_part):
                tt = slot * PEER_TB + t
                s = jnp.sum(lp, axis=-1, keepdims=True)
                act = 0.5 * s * (1.0 + lax.erf(s * (2.0 ** -0.5)))
                a = act * gates[:, tt:tt + 1]
                a_lanes.append(_dot_nt(lane0, jnp.broadcast_to(a, (PEER_SEL, LANES)).astype(BF16)))
            a_sel = [jnp.where(own_row, _dot(al.astype(BF16), expand), 0.0).astype(BF16) for al in a_lanes]
            for t, sel in zip(toks, a_sel):
                tt = slot * PEER_TB + t
                v = buf[t * PEER_SEL:(t + 1) * PEER_SEL, SU:, :].reshape(PEER_SEL * SU, LANES)
                y2 = _dot(sel, v)
                y = jnp.concatenate([y2[c:c + 1, :] for c in range(SU)], axis=1)
                o_ref[tt:tt + 1, :] = x_ref[tt:tt + 1, :] + y

    @pl.when(i == 0)
    def _():
        for s in range(PEER_LOOKAHEAD):
            start_fetch(idx_ref, s)

    for s in range(PEER_SLOTS):
        wait_slot(s)
        ahead = s + PEER_LOOKAHEAD
        compute(s, ahead % PEER_SLOTS, idx_ref if ahead < PEER_SLOTS else idx_next_ref)

    @pl.when(i == n - 1)
    def _():
        for s in range(PEER_LOOKAHEAD):
            wait_slot(s)


def peer_experts(idx, gate_t, h, x, uv):
    T, D = x.shape
    assert T % PEER_STEP == 0 and MXU_DEPTH % (D // LANES) == 0
    n = T // PEER_STEP
    slot_shape = (PEER_TB * PEER_SEL, 2 * D // LANES, LANES)
    per_tile = GATE_TILE // PEER_STEP
    return pl.pallas_call(
        functools.partial(_peer_expert_kernel, D=D),
        grid=(n,),
        in_specs=[pl.BlockSpec((PEER_STEP * PEER_SEL,), lambda i: (i,), memory_space=pltpu.SMEM),
                  pl.BlockSpec((PEER_STEP * PEER_SEL,), lambda i: (jnp.minimum(i + 1, n - 1),), memory_space=pltpu.SMEM),
                  pl.BlockSpec((PEER_STEP, D), lambda i: (i, 0)),
                  pl.BlockSpec((PEER_SEL, GATE_TILE), lambda i: (0, i // per_tile)),
                  pl.BlockSpec((PEER_STEP, D), lambda i: (i, 0)),
                  pl.BlockSpec(memory_space=pl.ANY)],
        out_specs=pl.BlockSpec((PEER_STEP, D), lambda i: (i, 0)),
        out_shape=jax.ShapeDtypeStruct((T, D), F32),
        scratch_shapes=[pltpu.VMEM(slot_shape, uv.dtype) for _ in range(PEER_SLOTS)]
                       + [pltpu.SemaphoreType.DMA((PEER_SLOTS,))],
        compiler_params=_cparams(("arbitrary",)),
    )(idx.reshape(-1), idx.reshape(-1), h, gate_t, x, uv)


def _ple_epilogue(acc, j, extra, out_ref):
    x_ref, p_ref, wp_ref = extra
    out_ref[...] = x_ref[...] + _sigmoid(acc) * _dot(p_ref[...], wp_ref[...])


FOX_SAMPLE_PAGES = 2


def _fox_sample_kernel(pt_ref, q_ref, ks_ref, vs_ref, lfs_ref, *rest, scale):
    cache_refs = rest[:3 * FOX_SAMPLE_PAGES]
    o_ref, m_sc, l_sc, acc_sc, carry_sc = rest[3 * FOX_SAMPLE_PAGES:]
    j = pl.program_id(1)
    NP = FOX_SAMPLE_PAGES
    kc_refs, vc_refs, lfc_refs = cache_refs[:NP], cache_refs[NP:2 * NP], cache_refs[2 * NP:]
    H, PS, HD = FOX_HEADS, kc_refs[0].shape[0], FOX_HD
    q = q_ref[0]

    @pl.when(j == 0)
    def _():
        m_sc[...] = jnp.sum(q * ks_ref[0], axis=-1, keepdims=True) * scale
        l_sc[...] = jnp.ones_like(l_sc)
        acc_sc[...] = vs_ref[0]
        carry_sc[...] = lfs_ref[0][:, 0:1]

    ones = jnp.ones((HD, LANES), BF16)
    pos3 = lax.broadcasted_iota(jnp.int32, (PS, H, LANES), 0)
    lane3 = lax.broadcasted_iota(jnp.int32, (PS, H, LANES), 2)
    diag = pos3 == lane3
    jj = lax.broadcasted_iota(jnp.int32, (PS, PS), 0)
    pp = lax.broadcasted_iota(jnp.int32, (PS, PS), 1)
    after = (jj > pp).astype(F32)
    qk = [_dot((kc[...] * q[None]).reshape(PS * H, HD).astype(BF16), ones).reshape(PS, H, LANES) for kc in kc_refs]
    lf_t = [lfc[0] for lfc in lfc_refs]
    carry = [carry_sc[...]]
    for a in range(NP):
        carry.append(carry[a] + jnp.sum(lf_t[a], axis=-1, keepdims=True))
    s_t = [jnp.sum(jnp.where(diag, qk[a], 0.0), axis=0) * scale + _dot(lf_t[a], after, HIGHEST) + carry[a]
           for a in range(NP)]
    m_old = m_sc[...]
    m_new = m_old
    for a in range(NP):
        m_new = jnp.maximum(m_new, jnp.max(s_t[a], axis=-1, keepdims=True))
    alpha = jnp.exp(m_old - m_new)
    p_t = [jnp.exp(s - m_new) for s in s_t]
    l_new = alpha * l_sc[...]
    for a in range(NP):
        l_new = l_new + jnp.sum(p_t[a], axis=-1, keepdims=True)
    l_sc[...] = l_new
    p3 = [_dot(jnp.where(diag, p[None], 0.0).reshape(PS * H, LANES).astype(BF16), ones).reshape(PS, H, LANES) for p in p_t]
    acc = alpha * acc_sc[...]
    for a in range(NP):
        acc = acc + jnp.sum(p3[a] * vc_refs[a][...], axis=0)
    acc_sc[...] = acc
    m_sc[...] = m_new
    carry_sc[...] = carry[NP]

    @pl.when(j == pl.num_programs(1) - 1)
    def _():
        o_ref[0] = (acc_sc[...] / l_sc[...]).astype(o_ref.dtype)


def fox_sample_attention(page_table, q, k_self, v_self, lf_self, cache_k, cache_v, cache_lf_t, layer):
    Bs, n_pages = page_table.shape
    PS = cache_k.shape[2]
    H, HD = FOX_HEADS, FOX_HD

    NP = FOX_SAMPLE_PAGES
    assert n_pages % NP == 0 and PS == LANES

    def kv_spec(a):
        return pl.BlockSpec((None, None, PS, H, HD),
                            lambda b, j, pt: (layer, pt[b, n_pages - 1 - (j * NP + a)], 0, 0, 0))

    def lf_spec(a):
        return pl.BlockSpec((1, H, PS), lambda b, j, pt: (pt[b, n_pages - 1 - (j * NP + a)], 0, 0))

    grid_spec = pltpu.PrefetchScalarGridSpec(
        num_scalar_prefetch=1,
        grid=(Bs, n_pages // NP),
        in_specs=[pl.BlockSpec((1, H, HD), lambda b, j, pt: (b, 0, 0)),
                  pl.BlockSpec((1, H, HD), lambda b, j, pt: (b, 0, 0)),
                  pl.BlockSpec((1, H, HD), lambda b, j, pt: (b, 0, 0)),
                  pl.BlockSpec((1, H, LANES), lambda b, j, pt: (b, 0, 0)),
                  *[kv_spec(a) for a in range(NP)], *[kv_spec(a) for a in range(NP)],
                  *[lf_spec(a) for a in range(NP)]],
        out_specs=pl.BlockSpec((1, H, HD), lambda b, j, pt: (b, 0, 0)),
        scratch_shapes=[pltpu.VMEM((H, 1), F32), pltpu.VMEM((H, 1), F32),
                        pltpu.VMEM((H, HD), F32), pltpu.VMEM((H, 1), F32)],
    )
    return pl.pallas_call(
        functools.partial(_fox_sample_kernel, scale=FOX_HD ** -0.5),
        grid_spec=grid_spec,
        out_shape=jax.ShapeDtypeStruct((Bs, H, HD), BF16),
        compiler_params=_cparams(("parallel", "arbitrary")),
    )(page_table, q, k_self, v_self, lf_self, *[cache_k] * NP, *[cache_v] * NP, *[cache_lf_t] * NP)


def _gdn_sample_kernel(x_ref, cs_ref, w_ref, sm_ref, z_ref, gn_ref, s_ref, o_ref, s_out_ref, conv_out_ref):
    x = x_ref[0]
    cs = cs_ref[0]
    w = w_ref[...]
    y = x * w[CONV_W - 1]
    for jx in range(CONV_W - 1):
        y = y + cs[jx] * w[jx]
    y = _silu(y)
    conv_out_ref[0, 0:CONV_W - 2] = cs[1:]
    conv_out_ref[0, CONV_W - 2] = x
    nh = GDN_HEADS
    normed = y * lax.rsqrt(jnp.sum(y * y, axis=-1, keepdims=True) + EPS)
    rowid = lax.broadcasted_iota(jnp.int32, normed.shape, 0)
    qk = jnp.where(rowid < nh, normed * (GDN_DK ** -0.5), normed)
    pad = jnp.zeros((LANES - 2 * nh, GDN_DK), F32)
    qk_t = jnp.concatenate([qk[:2 * nh], pad], axis=0).T
    sm = sm_ref[0]
    outs = []
    for h in range(nh):
        q_col = qk_t[:, h:h + 1]
        k_col = qk_t[:, nh + h:nh + h + 1]
        v_row = y[2 * nh + h:2 * nh + h + 1, :]
        g = sm[:, SM_G + h:SM_G + h + 1]
        beta = sm[:, SM_BETA + h:SM_BETA + h + 1]
        S = s_ref[0, h] * jnp.exp(g)
        kv = jnp.sum(k_col * S, axis=0, keepdims=True)
        S = S + k_col * ((v_row - kv) * beta)
        s_out_ref[0, h] = S
        outs.append(jnp.sum(q_col * S, axis=0, keepdims=True))
    o = jnp.concatenate(outs, axis=0)
    o_ref[0] = _gdn_out(o, z_ref[0], gn_ref[...]).astype(o_ref.dtype)


def gdn_sample(x24, conv_state, conv_w, sm, z, gn, state):
    Bs = x24.shape[0]
    nch = CONV_CH // LANES
    H = GDN_HEADS
    return pl.pallas_call(
        _gdn_sample_kernel,
        grid=(Bs,),
        in_specs=[pl.BlockSpec((1, nch, LANES), lambda b: (b, 0, 0)),
                  pl.BlockSpec((1, CONV_W - 1, nch, LANES), lambda b: (b, 0, 0, 0)),
                  pl.BlockSpec((CONV_W, nch, LANES), lambda b: (0, 0, 0)),
                  pl.BlockSpec((1, 1, LANES), lambda b: (b, 0, 0)),
                  pl.BlockSpec((1, H, GDN_DV), lambda b: (b, 0, 0)),
                  pl.BlockSpec((1, GDN_DV), lambda b: (0, 0)),
                  pl.BlockSpec((1, H, GDN_DK, GDN_DV), lambda b: (b, 0, 0, 0))],
        out_specs=[pl.BlockSpec((1, H, GDN_DV), lambda b: (b, 0, 0)),
                   pl.BlockSpec((1, H, GDN_DK, GDN_DV), lambda b: (b, 0, 0, 0)),
                   pl.BlockSpec((1, CONV_W - 1, nch, LANES), lambda b: (b, 0, 0, 0))],
        out_shape=[jax.ShapeDtypeStruct((Bs, H, GDN_DV), BF16),
                   jax.ShapeDtypeStruct((Bs, H, GDN_DK, GDN_DV), F32),
                   jax.ShapeDtypeStruct((Bs, CONV_W - 1, nch, LANES), F32)],
        compiler_params=_cparams(("parallel",)),
    )(x24, conv_state, conv_w, sm, z, gn, state)


PROJ_TN = 512


def _pack_layer(lw, D):
    w_in = lw['w_in']
    small = jnp.concatenate([w_in[:, OFF_FF:OFF_GQKV], w_in[:, OFF_GA:OFF_GB], w_in[:, OFF_GB:OFF_GZ]], axis=1)
    small = jnp.pad(small, ((0, 0), (0, PROJ_TN - small.shape[1])))
    w_proj = jnp.concatenate([w_in[:, OFF_FQ:OFF_FF], w_in[:, OFF_GQKV:OFF_GA], w_in[:, OFF_GZ:], small], axis=1).astype(BF16)
    n_main = P_GATE + 2 * D
    gains = jnp.concatenate([jnp.tile(lw['fox_q_norm'], FOX_HEADS), jnp.tile(lw['fox_k_norm'], FOX_HEADS),
                             jnp.zeros((n_main + PROJ_TN - 2 * FOX_W,), F32)])[None, :]
    zeros = jnp.zeros((LANES - SM_BETA,), F32)
    par = jnp.stack([jnp.concatenate([lw['fox_f_bias'], lw['gdn_dt_bias'], zeros]),
                     jnp.concatenate([jnp.zeros((SM_G,), F32), lw['gdn_a_log'], zeros])])
    par = jnp.pad(par, ((0, SUBLANES - 2), (0, 0)))
    return dict(
        w_proj=w_proj, gains=gains, par=par, n_main=n_main,
        norm_mix_g=lw['norm_mix_g'][None, :], conv_w=lw['gdn_conv_w'], gdn_norm_g=lw['gdn_norm_g'][None, :],
        w_up_fox=lw['w_up_fox'].astype(BF16), w_up_gdn=lw['w_up_gdn'].astype(BF16), w_out=lw['w_out'].astype(BF16),
        norm_ffn_g=lw['norm_ffn_g'][None, :], peer_w_q=lw['peer_w_q'].astype(BF16),
        sub_keys=lw['peer_sub_keys'].astype(BF16),
        uv=jnp.concatenate([lw['peer_u'].astype(BF16), lw['peer_v'].astype(BF16)], axis=1).reshape(-1, 2 * D // LANES, LANES),
        norm_ple_g=lw['norm_ple_g'][None, :], w_ple=lw['w_ple'].astype(BF16), w_ple_gate=lw['w_ple_gate'].astype(BF16),
    )


def _project(x, pk, tm):
    D = x.shape[1]
    tn = PROJ_TN
    n_qk = 2 * FOX_W // tn
    n_gate = 2 * D // tn
    n_plain = pk['n_main'] // tn - n_qk - n_gate
    epi = functools.partial(_proj_epilogue, tn=tn, n_qk=n_qk, n_plain=n_plain, n_gate=n_gate)
    return normed_linear(
        x, pk['norm_mix_g'], pk['w_proj'], tm=tm, tn=tn, epilogue=epi,
        extra=(pk['gains'], pk['par']),
        extra_specs=(pl.BlockSpec((1, tn), lambda i, j: (0, j)), pl.BlockSpec((SUBLANES, LANES), lambda i, j: (0, 0))))


def _channel_and_ple(x1, p, pk, tm):
    T, D = x1.shape
    Tp = -(-T // GATE_TILE) * GATE_TILE
    q, hn = normed_linear(x1, pk['norm_ffn_g'], pk['peer_w_q'], tm=tm, tn=512, epilogue=_plain_epilogue, out_dtype=BF16, emit_h=True)
    if Tp != T:
        q = jnp.pad(q, ((0, Tp - T), (0, 0)))
    idx_t, gate_t = peer_route(q, pk['sub_keys'], tb=512)
    idx = idx_t.reshape(PEER_SEL, Tp).T[:T]
    x2 = peer_experts(idx, gate_t.reshape(PEER_SEL, Tp), hn, x1, pk['uv'])
    tn = 512
    return normed_linear(
        x2, pk['norm_ple_g'], pk['w_ple_gate'], tm=tm, tn=tn, epilogue=_ple_epilogue,
        extra=(x2, p.astype(BF16), pk['w_ple']),
        extra_specs=(pl.BlockSpec((min(tm, T), tn), lambda i, j: (i, j)),
                     pl.BlockSpec((min(tm, T), p.shape[1]), lambda i, j: (i, 0)),
                     pl.BlockSpec((p.shape[1], tn), lambda i, j: (0, j))))


def _merge(x, o_fox, o_gdn, P, pk, tm):
    D = x.shape[1]
    m = merge_up(o_fox, o_gdn, pk['w_up_fox'], pk['w_up_gdn'], P, D, tm, 512)
    return linear_residual(m, pk['w_out'], x, tm, 512)


def _block_prompt(x3, p3, pk):
    B, L, D = x3.shape
    T = B * L
    x = x3.reshape(T, D)
    sm_block = pk['n_main'] // LANES
    P = _project(x, pk, tm=512)
    c_col = seq_cumsum(P, B, L, sm_block)
    c_row = c_col.reshape(B, L, LANES)[:, :, SM_LOGF:SM_LOGF + FOX_HEADS].transpose(0, 2, 1)
    o_fox = fox_prompt_attention(P, c_col, c_row, B, L)
    Y = gdn_prep(P, pk['conv_w'], B, L)
    o_gdn, s_new = gdn_chunked(Y, P, pk['gdn_norm_g'], B, L, sm_block)
    x1 = _merge(x, o_fox, o_gdn, P, pk, tm=512)
    y = _channel_and_ple(x1, p3.reshape(T, -1), pk, tm=512)
    k = P[:, P_K:P_K + FOX_W].reshape(B, L, FOX_HEADS, FOX_HD)
    v = P[:, P_V:P_V + FOX_W].reshape(B, L, FOX_HEADS, FOX_HD)
    logf = P[:, pk['n_main'] + SM_LOGF:pk['n_main'] + SM_LOGF + FOX_HEADS].reshape(B, L, FOX_HEADS)
    conv_new = P[:, P_GQKV:P_GQKV + CONV_CH].reshape(B, L, CONV_CH)[:, L - (CONV_W - 1):]
    return y.reshape(B, L, D), k, v, logf, s_new, conv_new


def _block_sample(x3, p3, cache_k, cache_v, cache_logf, s0, conv0, page_table, layer, pk):
    Bs, Ls, D = x3.shape
    assert Ls == 1
    x = x3.reshape(Bs, D)
    n_main = pk['n_main']
    nch = CONV_CH // LANES
    P = _project(x, pk, tm=Bs)
    q = P[:, P_Q:P_Q + FOX_W].reshape(Bs, FOX_HEADS, FOX_HD)
    k = P[:, P_K:P_K + FOX_W].reshape(Bs, FOX_HEADS, FOX_HD)
    v = P[:, P_V:P_V + FOX_W].reshape(Bs, FOX_HEADS, FOX_HD)
    sm = P[:, n_main:n_main + LANES]
    logf = sm[:, SM_LOGF:SM_LOGF + FOX_HEADS]
    lf_b = jnp.broadcast_to(logf[:, :, None], (Bs, FOX_HEADS, LANES))
    cache_lf_t = cache_logf[layer].transpose(0, 2, 1)
    o_fox = fox_sample_attention(page_table, q, k, v, lf_b, cache_k, cache_v, cache_lf_t, layer)
    x24 = P[:, P_GQKV:P_GQKV + CONV_CH].reshape(Bs, nch, LANES)
    z = P[:, P_Z:P_Z + GDN_VW].reshape(Bs, GDN_HEADS, GDN_DV)
    o_gdn, s_new, conv_new = gdn_sample(
        x24, conv0.reshape(Bs, CONV_W - 1, nch, LANES), pk['conv_w'].reshape(CONV_W, nch, LANES),
        sm.reshape(Bs, 1, LANES), z, pk['gdn_norm_g'], s0)
    x1 = _merge(x, o_fox.reshape(Bs, FOX_W), o_gdn.reshape(Bs, GDN_VW), P, pk, tm=Bs)
    y = _channel_and_ple(x1, p3.reshape(Bs, -1), pk, tm=Bs)
    return (y.reshape(Bs, 1, D), k.reshape(Bs, 1, FOX_HEADS, FOX_HD), v.reshape(Bs, 1, FOX_HEADS, FOX_HD),
            logf.reshape(Bs, 1, FOX_HEADS), s_new, conv_new.reshape(Bs, CONV_W - 1, CONV_CH))


def kernel(x_prompt, x_sample, p_prompt, p_sample, cache_fox_k, cache_fox_v, cache_fox_logf, state_gdn, state_conv, page_table, norm_mix_g, w_in, fox_f_bias, fox_q_norm, fox_k_norm, gdn_conv_w, gdn_a_log, gdn_dt_bias, gdn_norm_g, w_up_fox, w_up_gdn, w_out, norm_ffn_g, peer_w_q, peer_sub_keys, peer_u, peer_v, norm_ple_g, w_ple, w_ple_gate):
    depth = w_in.shape[0]
    D = x_prompt.shape[-1]
    xp, xs = x_prompt, x_sample
    outs = [[] for _ in range(10)]
    for i in range(depth):
        lw = {
            'norm_mix_g': norm_mix_g[i], 'w_in': w_in[i], 'fox_f_bias': fox_f_bias[i],
            'fox_q_norm': fox_q_norm[i], 'fox_k_norm': fox_k_norm[i], 'gdn_conv_w': gdn_conv_w[i],
            'gdn_a_log': gdn_a_log[i], 'gdn_dt_bias': gdn_dt_bias[i], 'gdn_norm_g': gdn_norm_g[i],
            'w_up_fox': w_up_fox[i], 'w_up_gdn': w_up_gdn[i], 'w_out': w_out[i], 'norm_ffn_g': norm_ffn_g[i],
            'peer_w_q': peer_w_q[i], 'peer_sub_keys': peer_sub_keys[i], 'peer_u': peer_u[i], 'peer_v': peer_v[i],
            'norm_ple_g': norm_ple_g[i], 'w_ple': w_ple[i], 'w_ple_gate': w_ple_gate[i],
        }
        pk = _pack_layer(lw, D)
        xp, kp, vp, lfp, sp, cp = _block_prompt(xp, p_prompt[i], pk)
        xs, ks_, vs_, lfs, ss, cs = _block_sample(xs, p_sample[i], cache_fox_k, cache_fox_v, cache_fox_logf,
                                                  state_gdn[i], state_conv[i], page_table, i, pk)
        for lst, val in zip(outs, (kp, vp, lfp, sp, cp, ks_, vs_, lfs, ss, cs)):
            lst.append(val)
    return (xp, xs) + tuple(jnp.stack(o) for o in outs)
```

```python
import functools
import math

import jax
import jax.numpy as jnp
from jax import lax
from jax.experimental import pallas as pl
from jax.experimental.pallas import tpu as pltpu

F32 = jnp.float32
BF16 = jnp.bfloat16
HIGHEST = lax.Precision.HIGHEST
EPS = 1e-6
NEG_BIG = -1e30

LANES = 128
SUBLANES = 8
MXU_DEPTH = 256
DMA_THREADS = 2
VMEM_LIMIT = 56 * 1024 * 1024

FOX_HEADS = 8
FOX_HD = 128
FOX_W = FOX_HEADS * FOX_HD
GDN_HEADS = 8
GDN_DK = 128
GDN_DV = 128
GDN_KW = GDN_HEADS * GDN_DK
GDN_VW = GDN_HEADS * GDN_DV
CONV_W = 4
CONV_CH = 2 * GDN_KW + GDN_VW
GDN_CHUNK = 64
PEER_HEADS = 8
N_KEYS = 128
PEER_DK = 128
PEER_TOPK = 16
PEER_SEL = PEER_HEADS * PEER_TOPK

OFF_FQ = 0
OFF_FK = OFF_FQ + FOX_W
OFF_FV = OFF_FK + FOX_W
OFF_FF = OFF_FV + FOX_W
OFF_GQKV = OFF_FF + FOX_HEADS
OFF_GA = OFF_GQKV + CONV_CH
OFF_GB = OFF_GA + GDN_HEADS
OFF_GZ = OFF_GB + GDN_HEADS
OFF_GATE = OFF_GZ + GDN_VW

P_Q = 0
P_K = P_Q + FOX_W
P_V = P_K + FOX_W
P_GQKV = P_V + FOX_W
P_Z = P_GQKV + CONV_CH
P_GATE = P_Z + GDN_VW
SM_LOGF = 0
SM_G = SM_LOGF + FOX_HEADS
SM_BETA = SM_G + GDN_HEADS


def _cparams(sem):
    return pltpu.CompilerParams(dimension_semantics=sem, vmem_limit_bytes=VMEM_LIMIT)


def _sigmoid(x):
    return 1.0 / (1.0 + jnp.exp(-x))


def _silu(x):
    return x * _sigmoid(x)


def _dot(a, b, precision=None):
    return jnp.dot(a, b, preferred_element_type=F32, precision=precision)


def _dot_nt(a, b, precision=None):
    return lax.dot_general(a, b, (((1,), (1,)), ((), ())), preferred_element_type=F32, precision=precision)


def _dot_tn(a, b, precision=None):
    return lax.dot_general(a, b, (((0,), (0,)), ((), ())), preferred_element_type=F32, precision=precision)


def _split(x):
    hi = x.astype(BF16)
    return hi, (x - hi.astype(F32)).astype(BF16)


def _mm3(dot, a, b):
    return dot(a[0], b[0]) + (dot(a[0], b[1]) + dot(a[1], b[0]))


def _normed_linear_kernel(*refs, n_extra, emit_h, epilogue):
    x_ref, g_ref, w_ref = refs[:3]
    extra = refs[3:3 + n_extra]
    out_ref = refs[3 + n_extra]
    h_out = refs[4 + n_extra] if emit_h else None
    h_sc = refs[-1]
    j = pl.program_id(1)

    @pl.when(j == 0)
    def _():
        x = x_ref[...]
        y = x * lax.rsqrt(jnp.mean(x * x, axis=-1, keepdims=True) + EPS) * g_ref[...]
        h_sc[...] = y.astype(BF16)
        if emit_h:
            h_out[...] = y.astype(h_out.dtype)

    acc = _dot(h_sc[...], w_ref[...])
    epilogue(acc, j, extra, out_ref)


def normed_linear(x, g, w, *, tm, tn, epilogue, extra=(), extra_specs=(), out_dtype=F32, emit_h=False):
    T, D = x.shape
    N = w.shape[1]
    tm = min(tm, T)
    tn = min(tn, N)
    assert T % tm == 0 and N % tn == 0
    out_shape = [jax.ShapeDtypeStruct((T, N), out_dtype)]
    out_specs = [pl.BlockSpec((tm, tn), lambda i, j: (i, j))]
    if emit_h:
        out_shape.append(jax.ShapeDtypeStruct((T, D), F32))
        out_specs.append(pl.BlockSpec((tm, D), lambda i, j: (i, 0)))
    res = pl.pallas_call(
        functools.partial(_normed_linear_kernel, n_extra=len(extra), emit_h=emit_h, epilogue=epilogue),
        grid=(T // tm, N // tn),
        in_specs=[pl.BlockSpec((tm, D), lambda i, j: (i, 0)),
                  pl.BlockSpec((1, D), lambda i, j: (0, 0)),
                  pl.BlockSpec((D, tn), lambda i, j: (0, j)),
                  *extra_specs],
        out_specs=out_specs,
        out_shape=out_shape,
        scratch_shapes=[pltpu.VMEM((tm, D), BF16)],
        compiler_params=_cparams(("parallel", "arbitrary")),
    )(x, g, w, *extra)
    return res if emit_h else res[0]


def _plain_epilogue(acc, j, extra, out_ref):
    out_ref[...] = acc.astype(out_ref.dtype)


def _proj_epilogue(acc, j, extra, out_ref, *, tn, n_qk, n_plain, n_gate):
    gain_ref, par_ref = extra

    @pl.when(j < n_qk)
    def _():
        for c in range(tn // LANES):
            blk = acc[:, c * LANES:(c + 1) * LANES]
            ms = jnp.mean(blk * blk, axis=-1, keepdims=True)
            out_ref[:, c * LANES:(c + 1) * LANES] = blk * lax.rsqrt(ms + EPS) * gain_ref[:, c * LANES:(c + 1) * LANES]

    @pl.when(jnp.logical_and(j >= n_qk, j < n_qk + n_plain))
    def _():
        out_ref[...] = acc

    @pl.when(jnp.logical_and(j >= n_qk + n_plain, j < n_qk + n_plain + n_gate))
    def _():
        out_ref[...] = _sigmoid(acc)

    @pl.when(j == n_qk + n_plain + n_gate)
    def _():
        v = acc[:, :LANES] + par_ref[0:1, :]
        lane = lax.broadcasted_iota(jnp.int32, v.shape, 1)
        tail = jnp.log1p(jnp.exp(-jnp.abs(v)))
        logf = -(jnp.maximum(-v, 0.0) + tail)
        g = -jnp.exp(par_ref[1:2, :]) * (jnp.maximum(v, 0.0) + tail)
        beta = _sigmoid(v)
        res = jnp.where(lane < SM_G, logf, jnp.where(lane < SM_BETA, g, beta))
        out_ref[:, :LANES] = res
        if tn > LANES:
            out_ref[:, LANES:] = jnp.zeros((acc.shape[0], tn - LANES), F32)


def _cumsum_kernel(sm_ref, out_ref, carry_sc, *, tr):
    @pl.when(pl.program_id(1) == 0)
    def _():
        carry_sc[...] = jnp.zeros_like(carry_sc)

    r = lax.broadcasted_iota(jnp.int32, (tr, tr), 0)
    c = lax.broadcasted_iota(jnp.int32, (tr, tr), 1)
    tri = (c <= r).astype(F32)
    cs = _dot(tri, sm_ref[...], HIGHEST) + carry_sc[...]
    out_ref[...] = cs
    carry_sc[...] = cs[tr - 1:tr, :]


def seq_cumsum(P, B, L, col_block):
    tr = min(256, L)
    nb = L // tr
    return pl.pallas_call(
        functools.partial(_cumsum_kernel, tr=tr),
        grid=(B, nb),
        in_specs=[pl.BlockSpec((tr, LANES), lambda b, i: (b * nb + i, col_block))],
        out_specs=pl.BlockSpec((tr, LANES), lambda b, i: (b * nb + i, 0)),
        out_shape=jax.ShapeDtypeStruct((B * L, LANES), F32),
        scratch_shapes=[pltpu.VMEM((1, LANES), F32)],
        compiler_params=_cparams(("parallel", "arbitrary")),
    )(P)


def _fox_prompt_kernel(q_ref, k_ref, v_ref, cq_ref, ck_ref, o_ref, m_sc, l_sc, acc_sc, *, tq, tk, scale):
    qi = pl.program_id(1)
    kj = pl.program_id(2)

    @pl.when(kj == 0)
    def _():
        m_sc[...] = jnp.full(m_sc.shape, NEG_BIG, F32)
        l_sc[...] = jnp.zeros_like(l_sc)
        acc_sc[...] = jnp.zeros_like(acc_sc)

    @pl.when(kj * tk <= qi * tq + tq - 1)
    def _():
        q_pos = qi * tq + lax.broadcasted_iota(jnp.int32, (tq, tk), 0)
        k_pos = kj * tk + lax.broadcasted_iota(jnp.int32, (tq, tk), 1)
        keep = k_pos <= q_pos
        cq = cq_ref[...]
        ck = ck_ref[0]
        for h in range(FOX_HEADS):
            sl = slice(h * FOX_HD, (h + 1) * FOX_HD)
            s = _dot_nt(q_ref[:, sl].astype(BF16), k_ref[:, sl].astype(BF16)) * scale
            s = s + (cq[:, h:h + 1] - ck[h:h + 1, :])
            s = jnp.where(keep, s, NEG_BIG)
            m_old = m_sc[h]
            m_new = jnp.maximum(m_old, jnp.max(s, axis=-1, keepdims=True))
            alpha = jnp.exp(m_old - m_new)
            p = jnp.exp(s - m_new)
            l_sc[h] = alpha * l_sc[h] + jnp.sum(p, axis=-1, keepdims=True)
            acc_sc[:, sl] = alpha * acc_sc[:, sl] + _dot(p.astype(BF16), v_ref[:, sl].astype(BF16))
            m_sc[h] = m_new

    @pl.when(kj == pl.num_programs(2) - 1)
    def _():
        for h in range(FOX_HEADS):
            sl = slice(h * FOX_HD, (h + 1) * FOX_HD)
            o_ref[:, sl] = (acc_sc[:, sl] / l_sc[h]).astype(o_ref.dtype)


def fox_prompt_attention(P, c_col, c_row, B, L):
    tq = min(256, L)
    tk = min(512, L)
    nq, nk = L // tq, L // tk

    def kv_block(b, qi, kj):
        last = (qi * tq + tq - 1) // tk
        return b * nk + jnp.minimum(kj, last)

    return pl.pallas_call(
        functools.partial(_fox_prompt_kernel, tq=tq, tk=tk, scale=FOX_HD ** -0.5),
        grid=(B, nq, nk),
        in_specs=[pl.BlockSpec((tq, FOX_W), lambda b, qi, kj: (b * nq + qi, P_Q // FOX_W)),
                  pl.BlockSpec((tk, FOX_W), lambda b, qi, kj: (kv_block(b, qi, kj), P_K // FOX_W)),
                  pl.BlockSpec((tk, FOX_W), lambda b, qi, kj: (kv_block(b, qi, kj), P_V // FOX_W)),
                  pl.BlockSpec((tq, LANES), lambda b, qi, kj: (b * nq + qi, 0)),
                  pl.BlockSpec((1, FOX_HEADS, tk), lambda b, qi, kj: (b, 0, jnp.minimum(kj, (qi * tq + tq - 1) // tk)))],
        out_specs=pl.BlockSpec((tq, FOX_W), lambda b, qi, kj: (b * nq + qi, 0)),
        out_shape=jax.ShapeDtypeStruct((B * L, FOX_W), BF16),
        scratch_shapes=[pltpu.VMEM((FOX_HEADS, tq, 1), F32),
                        pltpu.VMEM((FOX_HEADS, tq, 1), F32),
                        pltpu.VMEM((tq, FOX_W), F32)],
        compiler_params=_cparams(("parallel", "parallel", "arbitrary")),
    )(P, P, P, c_col, c_row)


def _gdn_prep_kernel(x_ref, w_ref, o_ref):
    c = pl.program_id(1)
    x = x_ref[...]
    w = w_ref[...]
    row = lax.broadcasted_iota(jnp.int32, x.shape, 0)
    y = x * w[CONV_W - 1:CONV_W, :]
    for s in range(1, CONV_W):
        shifted = jnp.where(row >= s, pltpu.roll(x, s, 0), 0.0)
        y = y + shifted * w[CONV_W - 1 - s:CONV_W - s, :]
    y = _silu(y)
    normed = y * lax.rsqrt(jnp.sum(y * y, axis=-1, keepdims=True) + EPS)
    q_scale = jnp.where(c < GDN_HEADS, GDN_DK ** -0.5, 1.0)
    o_ref[...] = jnp.where(c < 2 * GDN_HEADS, normed * q_scale, y)


def gdn_prep(P, conv_w, B, L):
    nch = CONV_CH // LANES
    return pl.pallas_call(
        _gdn_prep_kernel,
        grid=(B, nch),
        in_specs=[pl.BlockSpec((L, LANES), lambda b, c: (b, P_GQKV // LANES + c)),
                  pl.BlockSpec((CONV_W, LANES), lambda b, c: (0, c))],
        out_specs=pl.BlockSpec((L, LANES), lambda b, c: (b, c)),
        out_shape=jax.ShapeDtypeStruct((B * L, CONV_CH), F32),
        compiler_params=_cparams(("parallel", "parallel")),
    )(P, conv_w)


def _gdn_out(o, z, gn):
    y = o * lax.rsqrt(jnp.mean(o * o, axis=-1, keepdims=True) + EPS) * gn
    return y * _silu(z)


GDN_LOCAL_CHUNKS = 2


def _gdn_local_kernel(y_ref, sm_ref, u_ref, w_ref, qg_ref, kd_ref, attn_ref, eg_ref):
    C = GDN_CHUNK
    assert 2 * C == LANES
    r = lax.broadcasted_iota(jnp.int32, (C, LANES), 0)
    c = lax.broadcasted_iota(jnp.int32, (C, LANES), 1)
    incl = c <= r
    strict = c < r
    r_sq = lax.broadcasted_iota(jnp.int32, (C, C), 0)
    c_sq = lax.broadcasted_iota(jnp.int32, (C, C), 1)
    eye = (c_sq == r_sq).astype(F32)
    tril = (c_sq <= r_sq).astype(F32)
    zpad = jnp.zeros((LANES - C, LANES), F32)
    pw, t_inv, tail = [], [], []
    for ch in range(GDN_LOCAL_CHUNKS):
        rows = slice(ch * C, (ch + 1) * C)
        sm = sm_ref[rows, :]
        gcum_all = _dot(tril, sm, HIGHEST)
        gcum_pad = jnp.concatenate([gcum_all, zpad], axis=0)
        eg_ref[rows, :] = jnp.exp(jnp.broadcast_to(gcum_all[C - 1:C, :], (C, LANES)))
        for h in range(GDN_HEADS):
            hs = slice(h * GDN_DK, (h + 1) * GDN_DK)
            q = y_ref[rows, h * GDN_DK:(h + 1) * GDN_DK]
            k = y_ref[rows, GDN_KW + h * GDN_DK:GDN_KW + (h + 1) * GDN_DK]
            v = y_ref[rows, 2 * GDN_KW + h * GDN_DV:2 * GDN_KW + (h + 1) * GDN_DV]
            gc = gcum_all[:, SM_G + h:SM_G + h + 1]
            gr = _dot_nt((c == SM_G + h).astype(F32), gcum_pad, HIGHEST)
            beta = sm[:, SM_BETA + h:SM_BETA + h + 1]
            g_last = gc[C - 1:C, :]
            decay = jnp.where(incl, jnp.exp(jnp.where(incl, gc - gr, 0.0)), 0.0)
            kb = k * beta
            k2 = _split(jnp.concatenate([k, zpad], axis=0))
            a = jnp.where(strict, _mm3(_dot_nt, _split(kb), k2) * decay, 0.0)[:, :C]
            qg_ref[rows, hs] = q * jnp.exp(gc)
            kd_ref[rows, hs] = k * jnp.exp(g_last - gc)
            attn_ref[rows, h * LANES:(h + 1) * LANES] = _mm3(_dot_nt, _split(q), k2) * decay
            pw.append(-a)
            t_inv.append(eye - a)
            tail.append((rows, hs, _split(v * beta), _split(kb * jnp.exp(gc))))
    for _ in range(int(math.log2(C)) - 1):
        for n in range(len(pw)):
            pw2 = _split(pw[n])
            pw[n] = _mm3(_dot, pw2, pw2)
        for n in range(len(pw)):
            t_inv[n] = t_inv[n] + _mm3(_dot, _split(t_inv[n]), _split(pw[n]))
    for n, (rows, hs, vb2, kg2) in enumerate(tail):
        t2 = _split(t_inv[n])
        u_ref[rows, hs] = _mm3(_dot, t2, vb2)
        w_ref[rows, hs] = _mm3(_dot, t2, kg2)


def _gdn_scan_kernel(u_ref, w_ref, qg_ref, kd_ref, attn_ref, eg_ref, z_ref, gn_ref, o_ref, s_out_ref, s_sc):
    ci = pl.program_id(1)
    C = GDN_CHUNK

    @pl.when(ci == 0)
    def _():
        s_sc[...] = jnp.zeros_like(s_sc)

    gn = gn_ref[...]
    eg_row = eg_ref[...][0:1, :]
    heads = range(GDN_HEADS)
    hs = [slice(h * GDN_DV, (h + 1) * GDN_DV) for h in heads]
    S = [s_sc[h] for h in heads]
    S2 = [_split(S[h]) for h in heads]
    v2 = [_split(u_ref[:, hs[h]] - _mm3(_dot, _split(w_ref[:, hs[h]]), S2[h])) for h in heads]
    o_state = [_mm3(_dot, _split(qg_ref[:, hs[h]]), S2[h]) for h in heads]
    for h in heads:
        o = o_state[h] + _mm3(_dot, _split(attn_ref[:, h * LANES:h * LANES + C]), v2[h])
        o_ref[:, hs[h]] = _gdn_out(o, z_ref[:, hs[h]], gn).astype(o_ref.dtype)
    for h in heads:
        s_sc[h] = S[h] * eg_row[:, SM_G + h:SM_G + h + 1] + _mm3(_dot_tn, _split(kd_ref[:, hs[h]]), v2[h])

    @pl.when(ci == pl.num_programs(1) - 1)
    def _():
        s_out_ref[0] = s_sc[...]


def gdn_chunked(Y, P, gn, B, L, sm_block):
    C = GDN_CHUNK
    n = L // C
    T = B * L
    rows = C * GDN_LOCAL_CHUNKS
    assert T % rows == 0
    wide = jax.ShapeDtypeStruct((T, GDN_VW), F32)
    u, w, qg, kd, attn, eg = pl.pallas_call(
        _gdn_local_kernel,
        grid=(T // rows,),
        in_specs=[pl.BlockSpec((rows, CONV_CH), lambda i: (i, 0)),
                  pl.BlockSpec((rows, LANES), lambda i: (i, sm_block))],
        out_specs=[pl.BlockSpec((rows, GDN_VW), lambda i: (i, 0))] * 4
                  + [pl.BlockSpec((rows, GDN_HEADS * LANES), lambda i: (i, 0)), pl.BlockSpec((rows, LANES), lambda i: (i, 0))],
        out_shape=[wide] * 4 + [jax.ShapeDtypeStruct((T, GDN_HEADS * LANES), F32), jax.ShapeDtypeStruct((T, LANES), F32)],
        compiler_params=_cparams(("parallel",)),
    )(Y, P)
    chunk = lambda width: pl.BlockSpec((C, width), lambda b, i: (b * n + i, 0))
    return pl.pallas_call(
        _gdn_scan_kernel,
        grid=(B, n),
        in_specs=[chunk(GDN_VW), chunk(GDN_VW), chunk(GDN_VW), chunk(GDN_VW), chunk(GDN_HEADS * LANES), chunk(LANES),
                  pl.BlockSpec((C, GDN_VW), lambda b, i: (b * n + i, P_Z // GDN_VW)),
                  pl.BlockSpec((1, GDN_DV), lambda b, i: (0, 0))],
        out_specs=[pl.BlockSpec((C, GDN_VW), lambda b, i: (b * n + i, 0)),
                   pl.BlockSpec((1, GDN_HEADS, GDN_DK, GDN_DV), lambda b, i: (b, 0, 0, 0))],
        out_shape=[jax.ShapeDtypeStruct((T, GDN_VW), BF16),
                   jax.ShapeDtypeStruct((B, GDN_HEADS, GDN_DK, GDN_DV), F32)],
        scratch_shapes=[pltpu.VMEM((GDN_HEADS, GDN_DK, GDN_DV), F32)],
        compiler_params=_cparams(("parallel", "arbitrary")),
    )(u, w, qg, kd, attn, eg, P, gn)


def _merge_up_kernel(of_ref, og_ref, wf_ref, wg_ref, ga_ref, gb_ref, m_ref):
    m = ga_ref[...] * _dot(of_ref[...], wf_ref[...]) + gb_ref[...] * _dot(og_ref[...], wg_ref[...])
    m_ref[...] = m.astype(m_ref.dtype)


def merge_up(o_fox, o_gdn, w_up_fox, w_up_gdn, P, D, tm, tn):
    T = o_fox.shape[0]
    tm = min(tm, T)
    gate_block = P_GATE // tn
    return pl.pallas_call(
        _merge_up_kernel,
        grid=(T // tm, D // tn),
        in_specs=[pl.BlockSpec((tm, FOX_W), lambda i, j: (i, 0)),
                  pl.BlockSpec((tm, GDN_VW), lambda i, j: (i, 0)),
                  pl.BlockSpec((FOX_W, tn), lambda i, j: (0, j)),
                  pl.BlockSpec((GDN_VW, tn), lambda i, j: (0, j)),
                  pl.BlockSpec((tm, tn), lambda i, j: (i, gate_block + j)),
                  pl.BlockSpec((tm, tn), lambda i, j: (i, gate_block + D // tn + j))],
        out_specs=pl.BlockSpec((tm, tn), lambda i, j: (i, j)),
        out_shape=jax.ShapeDtypeStruct((T, D), BF16),
        compiler_params=_cparams(("parallel", "parallel")),
    )(o_fox, o_gdn, w_up_fox, w_up_gdn, P, P)


def _linear_residual_kernel(a_ref, w_ref, x_ref, o_ref):
    o_ref[...] = x_ref[...] + _dot(a_ref[...], w_ref[...])


def linear_residual(a, w, x, tm, tn):
    T, K = a.shape
    N = w.shape[1]
    tm = min(tm, T)
    return pl.pallas_call(
        _linear_residual_kernel,
        grid=(T // tm, N // tn),
        in_specs=[pl.BlockSpec((tm, K), lambda i, j: (i, 0)),
                  pl.BlockSpec((K, tn), lambda i, j: (0, j)),
                  pl.BlockSpec((tm, tn), lambda i, j: (i, j))],
        out_specs=pl.BlockSpec((tm, tn), lambda i, j: (i, j)),
        out_shape=jax.ShapeDtypeStruct((T, N), F32),
        compiler_params=_cparams(("parallel", "parallel")),
    )(a, w, x)


def _top16(s, ids, val_sc, idx_sc):
    for it in range(PEER_TOPK):
        m = jnp.max(s, axis=0, keepdims=True)
        idx = jnp.min(jnp.where(s == m, ids, NO_ID), axis=0, keepdims=True)
        val_sc[it:it + 1, :] = m
        idx_sc[it:it + 1, :] = idx
        s = jnp.where(ids == idx, -jnp.inf, s)


NO_ID = 1e9
PAIR_COUNTS = tuple(PEER_TOPK // (i + 1) for i in range(PEER_TOPK))
PAIR_STARTS = tuple(sum(PAIR_COUNTS[:i]) for i in range(PEER_TOPK))
N_PAIRS = sum(PAIR_COUNTS)
N_PAIR_ROWS = -(-N_PAIRS // SUBLANES) * SUBLANES


def _peer_route_kernel(q_ref, keys_ref, idx_ref, gate_ref, va_sc, ia_sc, vb_sc, ib_sc, vc_sc, ic_sc, cand_sc):
    K = PEER_TOPK
    tb = q_ref.shape[0]
    key_ids = lax.broadcasted_iota(jnp.int32, (N_KEYS, tb), 0).astype(F32)
    sa = _dot_nt(keys_ref[0, 0], q_ref[:, :PEER_DK])
    _top16(sa, key_ids, va_sc, ia_sc)
    sb = _dot_nt(keys_ref[0, 1], q_ref[:, PEER_DK:])
    _top16(sb, key_ids, vb_sc, ib_sc)
    cand_sc[N_PAIR_ROWS - SUBLANES:, :] = jnp.full((SUBLANES, tb), -jnp.inf, F32)
    row = lax.broadcasted_iota(jnp.int32, (N_PAIR_ROWS, tb), 0)
    shift = jnp.zeros((N_PAIR_ROWS, tb), jnp.int32)
    for i in range(K):
        cand_sc[PAIR_STARTS[i]:PAIR_STARTS[i] + PAIR_COUNTS[i], :] = va_sc[i:i + 1, :] + vb_sc[0:PAIR_COUNTS[i], :]
        if i:
            shift = jnp.where(row >= PAIR_STARTS[i], i * K - PAIR_STARTS[i], shift)
    pair_ids = jnp.where(row < N_PAIRS, (row + shift).astype(F32), NO_ID)
    _top16(cand_sc[...], pair_ids, vc_sc, ic_sc)
    cv, ci = vc_sc[...], ic_sc[...]
    hi = jnp.floor(ci * (1.0 / K))
    lo = ci - hi * K
    ia_all, ib_all = ia_sc[...], ib_sc[...]
    ia = jnp.zeros_like(ci)
    ib = jnp.zeros_like(ci)
    for i in range(K):
        ia = jnp.where(hi == i, ia_all[i:i + 1, :], ia)
        ib = jnp.where(lo == i, ib_all[i:i + 1, :], ib)
    idx_ref[0] = (ia * N_KEYS + ib).astype(jnp.int32)
    e = jnp.exp(cv - cv[0:1, :])
    gate_ref[0] = e / jnp.sum(e, axis=0, keepdims=True)


def peer_route(q, sub_keys, tb):
    T = q.shape[0]
    tb = min(tb, T)
    K = PEER_TOPK
    return pl.pallas_call(
        _peer_route_kernel,
        grid=(T // tb, PEER_HEADS),
        in_specs=[pl.BlockSpec((tb, 2 * PEER_DK), lambda i, h: (i, h)),
                  pl.BlockSpec((1, 2, N_KEYS, PEER_DK), lambda i, h: (h, 0, 0, 0))],
        out_specs=[pl.BlockSpec((1, K, tb), lambda i, h: (h, 0, i)),
                   pl.BlockSpec((1, K, tb), lambda i, h: (h, 0, i))],
        out_shape=[jax.ShapeDtypeStruct((PEER_HEADS, K, T), jnp.int32),
                   jax.ShapeDtypeStruct((PEER_HEADS, K, T), F32)],
        scratch_shapes=[pltpu.VMEM((K, tb), F32) for _ in range(6)] + [pltpu.VMEM((N_PAIR_ROWS, tb), F32)],
        compiler_params=_cparams(("parallel", "parallel")),
    )(q, sub_keys)


PEER_TB = 8
PEER_SLOTS = 4
PEER_LOOKAHEAD = 2
PEER_STEP = PEER_TB * PEER_SLOTS
PEER_ISSUE_UNROLL = 8
PEER_INTERLEAVE = 8
GATE_TILE = 128


def _peer_expert_kernel(idx_ref, idx_next_ref, h_ref, gate_ref, x_ref, uv_hbm, o_ref, *scratch, D):
    i = pl.program_id(0)
    n = pl.num_programs(0)
    S = 2 * D // LANES
    SU = D // LANES
    bufs, sem = scratch[:PEER_SLOTS], scratch[PEER_SLOTS]
    slot_experts = PEER_TB * PEER_SEL

    def start_fetch(idx, slot):
        for t in range(PEER_TB):
            def body(kb, carry, t=t):
                for kk in range(PEER_ISSUE_UNROLL):
                    k = kb * PEER_ISSUE_UNROLL + kk
                    pltpu.make_async_copy(uv_hbm.at[idx[(slot * PEER_TB + t) * PEER_SEL + k]],
                                          bufs[slot].at[t * PEER_SEL + k], sem.at[slot]).start()
                return carry
            lax.fori_loop(0, PEER_SEL // PEER_ISSUE_UNROLL, body, 0)

    def wait_slot(slot):
        pltpu.make_async_copy(uv_hbm.at[pl.ds(0, slot_experts)], bufs[slot], sem.at[slot]).wait()

    tok = lax.broadcasted_iota(jnp.int32, (GATE_TILE, PEER_STEP), 0)
    col = lax.broadcasted_iota(jnp.int32, (GATE_TILE, PEER_STEP), 1)
    first = (i % (GATE_TILE // PEER_STEP)) * PEER_STEP
    onehot = (tok == first + col).astype(F32)
    gates = _dot(gate_ref[...], onehot, HIGHEST)

    def iota2(shape, axis):
        return lax.broadcasted_iota(jnp.int32, shape, axis)

    grp = MXU_DEPTH // SU
    row_group_sum = (iota2((grp, MXU_DEPTH), 1) // SU == iota2((grp, MXU_DEPTH), 0)).astype(BF16)
    lane0 = (iota2((SU, LANES), 1) == 0).astype(BF16)
    expand = (iota2((PEER_SEL, PEER_SEL * SU), 1) // SU == iota2((PEER_SEL, PEER_SEL * SU), 0)).astype(BF16)
    own_row = iota2((SU, PEER_SEL * SU), 1) % SU == iota2((SU, PEER_SEL * SU), 0)

    def compute(slot, ahead, ahead_idx):
        buf = bufs[slot]
        for t0 in range(0, PEER_TB, PEER_INTERLEAVE):
            toks = range(t0, t0 + PEER_INTERLEAVE)
            for t in toks:
                for k in range(PEER_SEL):
                    pltpu.make_async_copy(uv_hbm.at[ahead_idx[(ahead * PEER_TB + t) * PEER_SEL + k]],
                                          bufs[ahead].at[t * PEER_SEL + k], sem.at[ahead]).start(
                                              priority=k % DMA_THREADS)
            lane_part = []
            for t in toks:
                tt = slot * PEER_TB + t
                h = h_ref[tt:tt + 1, :]
                h2 = jnp.concatenate([h[:, c * LANES:(c + 1) * LANES] for c in range(SU)], axis=0).astype(BF16)
                prod = (buf[t * PEER_SEL:(t + 1) * PEER_SEL, :SU, :] * h2[None]).reshape(PEER_SEL * SU, LANES)
                lane_part.append(jnp.concatenate(
                    [_dot(row_group_sum, prod[g * MXU_DEPTH:(g + 1) * MXU_DEPTH, :]) for g in range(PEER_SEL // grp)],
                    axis=0))
            a_lanes = []
            for t, lp in zip(toks, lane_part):
                tt = slot * PEER_TB + t
                s = jnp.sum(lp, axis=-1, keepdims=True)
                act = 0.5 * s * (1.0 + lax.erf(s * (2.0 ** -0.5)))
                a = act * gates[:, tt:tt + 1]
                a_lanes.append(_dot_nt(lane0, jnp.broadcast_to(a, (PEER_SEL, LANES)).astype(BF16)))
            a_sel = [jnp.where(own_row, _dot(al.astype(BF16), expand), 0.0).astype(BF16) for al in a_lanes]
            for t, sel in zip(toks, a_sel):
                tt = slot * PEER_TB + t
                v = buf[t * PEER_SEL:(t + 1) * PEER_SEL, SU:, :].reshape(PEER_SEL * SU, LANES)
                y2 = _dot(sel, v)
                y = jnp.concatenate([y2[c:c + 1, :] for c in range(SU)], axis=1)
                o_ref[tt:tt + 1, :] = x_ref[tt:tt + 1, :] + y

    @pl.when(i == 0)
    def _():
        for s in range(PEER_LOOKAHEAD):
            start_fetch(idx_ref, s)

    for s in range(PEER_SLOTS):
        wait_slot(s)
        ahead = s + PEER_LOOKAHEAD
        compute(s, ahead % PEER_SLOTS, idx_ref if ahead < PEER_SLOTS else idx_next_ref)

    @pl.when(i == n - 1)
    def _():
        for s in range(PEER_LOOKAHEAD):
            wait_slot(s)


def peer_experts(idx, gate_t, h, x, uv):
    T, D = x.shape
    assert T % PEER_STEP == 0 and MXU_DEPTH % (D // LANES) == 0
    n = T // PEER_STEP
    slot_shape = (PEER_TB * PEER_SEL, 2 * D // LANES, LANES)
    per_tile = GATE_TILE // PEER_STEP
    return pl.pallas_call(
        functools.partial(_peer_expert_kernel, D=D),
        grid=(n,),
        in_specs=[pl.BlockSpec((PEER_STEP * PEER_SEL,), lambda i: (i,), memory_space=pltpu.SMEM),
                  pl.BlockSpec((PEER_STEP * PEER_SEL,), lambda i: (jnp.minimum(i + 1, n - 1),), memory_space=pltpu.SMEM),
                  pl.BlockSpec((PEER_STEP, D), lambda i: (i, 0)),
                  pl.BlockSpec((PEER_SEL, GATE_TILE), lambda i: (0, i // per_tile)),
                  pl.BlockSpec((PEER_STEP, D), lambda i: (i, 0)),
                  pl.BlockSpec(memory_space=pl.ANY)],
        out_specs=pl.BlockSpec((PEER_STEP, D), lambda i: (i, 0)),
        out_shape=jax.ShapeDtypeStruct((T, D), F32),
        scratch_shapes=[pltpu.VMEM(slot_shape, uv.dtype) for _ in range(PEER_SLOTS)]
                       + [pltpu.SemaphoreType.DMA((PEER_SLOTS,))],
        compiler_params=_cparams(("arbitrary",)),
    )(idx.reshape(-1), idx.reshape(-1), h, gate_t, x, uv)


def _ple_epilogue(acc, j, extra, out_ref):
    x_ref, p_ref, wp_ref = extra
    out_ref[...] = x_ref[...] + _sigmoid(acc) * _dot(p_ref[...], wp_ref[...])


FOX_SAMPLE_PAGES = 2


def _fox_sample_kernel(pt_ref, q_ref, ks_ref, vs_ref, lfs_ref, *rest, scale):
    cache_refs = rest[:3 * FOX_SAMPLE_PAGES]
    o_ref, m_sc, l_sc, acc_sc, carry_sc = rest[3 * FOX_SAMPLE_PAGES:]
    j = pl.program_id(1)
    NP = FOX_SAMPLE_PAGES
    kc_refs, vc_refs, lfc_refs = cache_refs[:NP], cache_refs[NP:2 * NP], cache_refs[2 * NP:]
    H, PS, HD = FOX_HEADS, kc_refs[0].shape[0], FOX_HD
    q = q_ref[0]

    @pl.when(j == 0)
    def _():
        m_sc[...] = jnp.sum(q * ks_ref[0], axis=-1, keepdims=True) * scale
        l_sc[...] = jnp.ones_like(l_sc)
        acc_sc[...] = vs_ref[0]
        carry_sc[...] = lfs_ref[0][:, 0:1]

    ones = jnp.ones((HD, LANES), BF16)
    pos3 = lax.broadcasted_iota(jnp.int32, (PS, H, LANES), 0)
    lane3 = lax.broadcasted_iota(jnp.int32, (PS, H, LANES), 2)
    diag = pos3 == lane3
    jj = lax.broadcasted_iota(jnp.int32, (PS, PS), 0)
    pp = lax.broadcasted_iota(jnp.int32, (PS, PS), 1)
    after = (jj > pp).astype(F32)
    qk = [_dot((kc[...] * q[None]).reshape(PS * H, HD).astype(BF16), ones).reshape(PS, H, LANES) for kc in kc_refs]
    lf_t = [lfc[0] for lfc in lfc_refs]
    carry = [carry_sc[...]]
    for a in range(NP):
        carry.append(carry[a] + jnp.sum(lf_t[a], axis=-1, keepdims=True))
    s_t = [jnp.sum(jnp.where(diag, qk[a], 0.0), axis=0) * scale + _dot(lf_t[a], after, HIGHEST) + carry[a]
           for a in range(NP)]
    m_old = m_sc[...]
    m_new = m_old
    for a in range(NP):
        m_new = jnp.maximum(m_new, jnp.max(s_t[a], axis=-1, keepdims=True))
    alpha = jnp.exp(m_old - m_new)
    p_t = [jnp.exp(s - m_new) for s in s_t]
    l_new = alpha * l_sc[...]
    for a in range(NP):
        l_new = l_new + jnp.sum(p_t[a], axis=-1, keepdims=True)
    l_sc[...] = l_new
    p3 = [_dot(jnp.where(diag, p[None], 0.0).reshape(PS * H, LANES).astype(BF16), ones).reshape(PS, H, LANES) for p in p_t]
    acc = alpha * acc_sc[...]
    for a in range(NP):
        acc = acc + jnp.sum(p3[a] * vc_refs[a][...], axis=0)
    acc_sc[...] = acc
    m_sc[...] = m_new
    carry_sc[...] = carry[NP]

    @pl.when(j == pl.num_programs(1) - 1)
    def _():
        o_ref[0] = (acc_sc[...] / l_sc[...]).astype(o_ref.dtype)


def fox_sample_attention(page_table, q, k_self, v_self, lf_self, cache_k, cache_v, cache_lf_t, layer):
    Bs, n_pages = page_table.shape
    PS = cache_k.shape[2]
    H, HD = FOX_HEADS, FOX_HD

    NP = FOX_SAMPLE_PAGES
    assert n_pages % NP == 0 and PS == LANES

    def kv_spec(a):
        return pl.BlockSpec((None, None, PS, H, HD),
                            lambda b, j, pt: (layer, pt[b, n_pages - 1 - (j * NP + a)], 0, 0, 0))

    def lf_spec(a):
        return pl.BlockSpec((1, H, PS), lambda b, j, pt: (pt[b, n_pages - 1 - (j * NP + a)], 0, 0))

    grid_spec = pltpu.PrefetchScalarGridSpec(
        num_scalar_prefetch=1,
        grid=(Bs, n_pages // NP),
        in_specs=[pl.BlockSpec((1, H, HD), lambda b, j, pt: (b, 0, 0)),
                  pl.BlockSpec((1, H, HD), lambda b, j, pt: (b, 0, 0)),
                  pl.BlockSpec((1, H, HD), lambda b, j, pt: (b, 0, 0)),
                  pl.BlockSpec((1, H, LANES), lambda b, j, pt: (b, 0, 0)),
                  *[kv_spec(a) for a in range(NP)], *[kv_spec(a) for a in range(NP)],
                  *[lf_spec(a) for a in range(NP)]],
        out_specs=pl.BlockSpec((1, H, HD), lambda b, j, pt: (b, 0, 0)),
        scratch_shapes=[pltpu.VMEM((H, 1), F32), pltpu.VMEM((H, 1), F32),
                        pltpu.VMEM((H, HD), F32), pltpu.VMEM((H, 1), F32)],
    )
    return pl.pallas_call(
        functools.partial(_fox_sample_kernel, scale=FOX_HD ** -0.5),
        grid_spec=grid_spec,
        out_shape=jax.ShapeDtypeStruct((Bs, H, HD), BF16),
        compiler_params=_cparams(("parallel", "arbitrary")),
    )(page_table, q, k_self, v_self, lf_self, *[cache_k] * NP, *[cache_v] * NP, *[cache_lf_t] * NP)


def _gdn_sample_kernel(x_ref, cs_ref, w_ref, sm_ref, z_ref, gn_ref, s_ref, o_ref, s_out_ref, conv_out_ref):
    x = x_ref[0]
    cs = cs_ref[0]
    w = w_ref[...]
    y = x * w[CONV_W - 1]
    for jx in range(CONV_W - 1):
        y = y + cs[jx] * w[jx]
    y = _silu(y)
    conv_out_ref[0, 0:CONV_W - 2] = cs[1:]
    conv_out_ref[0, CONV_W - 2] = x
    nh = GDN_HEADS
    normed = y * lax.rsqrt(jnp.sum(y * y, axis=-1, keepdims=True) + EPS)
    rowid = lax.broadcasted_iota(jnp.int32, normed.shape, 0)
    qk = jnp.where(rowid < nh, normed * (GDN_DK ** -0.5), normed)
    pad = jnp.zeros((LANES - 2 * nh, GDN_DK), F32)
    qk_t = jnp.concatenate([qk[:2 * nh], pad], axis=0).T
    sm = sm_ref[0]
    outs = []
    for h in range(nh):
        q_col = qk_t[:, h:h + 1]
        k_col = qk_t[:, nh + h:nh + h + 1]
        v_row = y[2 * nh + h:2 * nh + h + 1, :]
        g = sm[:, SM_G + h:SM_G + h + 1]
        beta = sm[:, SM_BETA + h:SM_BETA + h + 1]
        S = s_ref[0, h] * jnp.exp(g)
        kv = jnp.sum(k_col * S, axis=0, keepdims=True)
        S = S + k_col * ((v_row - kv) * beta)
        s_out_ref[0, h] = S
        outs.append(jnp.sum(q_col * S, axis=0, keepdims=True))
    o = jnp.concatenate(outs, axis=0)
    o_ref[0] = _gdn_out(o, z_ref[0], gn_ref[...]).astype(o_ref.dtype)


def gdn_sample(x24, conv_state, conv_w, sm, z, gn, state):
    Bs = x24.shape[0]
    nch = CONV_CH // LANES
    H = GDN_HEADS
    return pl.pallas_call(
        _gdn_sample_kernel,
        grid=(Bs,),
        in_specs=[pl.BlockSpec((1, nch, LANES), lambda b: (b, 0, 0)),
                  pl.BlockSpec((1, CONV_W - 1, nch, LANES), lambda b: (b, 0, 0, 0)),
                  pl.BlockSpec((CONV_W, nch, LANES), lambda b: (0, 0, 0)),
                  pl.BlockSpec((1, 1, LANES), lambda b: (b, 0, 0)),
                  pl.BlockSpec((1, H, GDN_DV), lambda b: (b, 0, 0)),
                  pl.BlockSpec((1, GDN_DV), lambda b: (0, 0)),
                  pl.BlockSpec((1, H, GDN_DK, GDN_DV), lambda b: (b, 0, 0, 0))],
        out_specs=[pl.BlockSpec((1, H, GDN_DV), lambda b: (b, 0, 0)),
                   pl.BlockSpec((1, H, GDN_DK, GDN_DV), lambda b: (b, 0, 0, 0)),
                   pl.BlockSpec((1, CONV_W - 1, nch, LANES), lambda b: (b, 0, 0, 0))],
        out_shape=[jax.ShapeDtypeStruct((Bs, H, GDN_DV), BF16),
                   jax.ShapeDtypeStruct((Bs, H, GDN_DK, GDN_DV), F32),
                   jax.ShapeDtypeStruct((Bs, CONV_W - 1, nch, LANES), F32)],
        compiler_params=_cparams(("parallel",)),
    )(x24, conv_state, conv_w, sm, z, gn, state)


PROJ_TN = 512


def _pack_layer(lw, D):
    w_in = lw['w_in']
    small = jnp.concatenate([w_in[:, OFF_FF:OFF_GQKV], w_in[:, OFF_GA:OFF_GB], w_in[:, OFF_GB:OFF_GZ]], axis=1)
    small = jnp.pad(small, ((0, 0), (0, PROJ_TN - small.shape[1])))
    w_proj = jnp.concatenate([w_in[:, OFF_FQ:OFF_FF], w_in[:, OFF_GQKV:OFF_GA], w_in[:, OFF_GZ:], small], axis=1).astype(BF16)
    n_main = P_GATE + 2 * D
    gains = jnp.concatenate([jnp.tile(lw['fox_q_norm'], FOX_HEADS), jnp.tile(lw['fox_k_norm'], FOX_HEADS),
                             jnp.zeros((n_main + PROJ_TN - 2 * FOX_W,), F32)])[None, :]
    zeros = jnp.zeros((LANES - SM_BETA,), F32)
    par = jnp.stack([jnp.concatenate([lw['fox_f_bias'], lw['gdn_dt_bias'], zeros]),
                     jnp.concatenate([jnp.zeros((SM_G,), F32), lw['gdn_a_log'], zeros])])
    par = jnp.pad(par, ((0, SUBLANES - 2), (0, 0)))
    return dict(
        w_proj=w_proj, gains=gains, par=par, n_main=n_main,
        norm_mix_g=lw['norm_mix_g'][None, :], conv_w=lw['gdn_conv_w'], gdn_norm_g=lw['gdn_norm_g'][None, :],
        w_up_fox=lw['w_up_fox'].astype(BF16), w_up_gdn=lw['w_up_gdn'].astype(BF16), w_out=lw['w_out'].astype(BF16),
        norm_ffn_g=lw['norm_ffn_g'][None, :], peer_w_q=lw['peer_w_q'].astype(BF16),
        sub_keys=lw['peer_sub_keys'].astype(BF16),
        uv=jnp.concatenate([lw['peer_u'].astype(BF16), lw['peer_v'].astype(BF16)], axis=1).reshape(-1, 2 * D // LANES, LANES),
        norm_ple_g=lw['norm_ple_g'][None, :], w_ple=lw['w_ple'].astype(BF16), w_ple_gate=lw['w_ple_gate'].astype(BF16),
    )


def _project(x, pk, tm):
    D = x.shape[1]
    tn = PROJ_TN
    n_qk = 2 * FOX_W // tn
    n_gate = 2 * D // tn
    n_plain = pk['n_main'] // tn - n_qk - n_gate
    epi = functools.partial(_proj_epilogue, tn=tn, n_qk=n_qk, n_plain=n_plain, n_gate=n_gate)
    return normed_linear(
        x, pk['norm_mix_g'], pk['w_proj'], tm=tm, tn=tn, epilogue=epi,
        extra=(pk['gains'], pk['par']),
        extra_specs=(pl.BlockSpec((1, tn), lambda i, j: (0, j)), pl.BlockSpec((SUBLANES, LANES), lambda i, j: (0, 0))))


def _channel_and_ple(x1, p, pk, tm):
    T, D = x1.shape
    Tp = -(-T // GATE_TILE) * GATE_TILE
    q, hn = normed_linear(x1, pk['norm_ffn_g'], pk['peer_w_q'], tm=tm, tn=512, epilogue=_plain_epilogue, out_dtype=BF16, emit_h=True)
    if Tp != T:
        q = jnp.pad(q, ((0, Tp - T), (0, 0)))
    idx_t, gate_t = peer_route(q, pk['sub_keys'], tb=512)
    idx = idx_t.reshape(PEER_SEL, Tp).T[:T]
    x2 = peer_experts(idx, gate_t.reshape(PEER_SEL, Tp), hn, x1, pk['uv'])
    tn = 512
    return normed_linear(
        x2, pk['norm_ple_g'], pk['w_ple_gate'], tm=tm, tn=tn, epilogue=_ple_epilogue,
        extra=(x2, p.astype(BF16), pk['w_ple']),
        extra_specs=(pl.BlockSpec((min(tm, T), tn), lambda i, j: (i, j)),
                     pl.BlockSpec((min(tm, T), p.shape[1]), lambda i, j: (i, 0)),
                     pl.BlockSpec((p.shape[1], tn), lambda i, j: (0, j))))


def _merge(x, o_fox, o_gdn, P, pk, tm):
    D = x.shape[1]
    m = merge_up(o_fox, o_gdn, pk['w_up_fox'], pk['w_up_gdn'], P, D, tm, 512)
    return linear_residual(m, pk['w_out'], x, tm, 512)


def _block_prompt(x3, p3, pk):
    B, L, D = x3.shape
    T = B * L
    x = x3.reshape(T, D)
    sm_block = pk['n_main'] // LANES
    P = _project(x, pk, tm=512)
    c_col = seq_cumsum(P, B, L, sm_block)
    c_row = c_col.reshape(B, L, LANES)[:, :, SM_LOGF:SM_LOGF + FOX_HEADS].transpose(0, 2, 1)
    o_fox = fox_prompt_attention(P, c_col, c_row, B, L)
    Y = gdn_prep(P, pk['conv_w'], B, L)
    o_gdn, s_new = gdn_chunked(Y, P, pk['gdn_norm_g'], B, L, sm_block)
    x1 = _merge(x, o_fox, o_gdn, P, pk, tm=512)
    y = _channel_and_ple(x1, p3.reshape(T, -1), pk, tm=512)
    k = P[:, P_K:P_K + FOX_W].reshape(B, L, FOX_HEADS, FOX_HD)
    v = P[:, P_V:P_V + FOX_W].reshape(B, L, FOX_HEADS, FOX_HD)
    logf = P[:, pk['n_main'] + SM_LOGF:pk['n_main'] + SM_LOGF + FOX_HEADS].reshape(B, L, FOX_HEADS)
    conv_new = P[:, P_GQKV:P_GQKV + CONV_CH].reshape(B, L, CONV_CH)[:, L - (CONV_W - 1):]
    return y.reshape(B, L, D), k, v, logf, s_new, conv_new


def _block_sample(x3, p3, cache_k, cache_v, cache_logf, s0, conv0, page_table, layer, pk):
    Bs, Ls, D = x3.shape
    assert Ls == 1
    x = x3.reshape(Bs, D)
    n_main = pk['n_main']
    nch = CONV_CH // LANES
    P = _project(x, pk, tm=Bs)
    q = P[:, P_Q:P_Q + FOX_W].reshape(Bs, FOX_HEADS, FOX_HD)
    k = P[:, P_K:P_K + FOX_W].reshape(Bs, FOX_HEADS, FOX_HD)
    v = P[:, P_V:P_V + FOX_W].reshape(Bs, FOX_HEADS, FOX_HD)
    sm = P[:, n_main:n_main + LANES]
    logf = sm[:, SM_LOGF:SM_LOGF + FOX_HEADS]
    lf_b = jnp.broadcast_to(logf[:, :, None], (Bs, FOX_HEADS, LANES))
    cache_lf_t = cache_logf[layer].transpose(0, 2, 1)
    o_fox = fox_sample_attention(page_table, q, k, v, lf_b, cache_k, cache_v, cache_lf_t, layer)
    x24 = P[:, P_GQKV:P_GQKV + CONV_CH].reshape(Bs, nch, LANES)
    z = P[:, P_Z:P_Z + GDN_VW].reshape(Bs, GDN_HEADS, GDN_DV)
    o_gdn, s_new, conv_new = gdn_sample(
        x24, conv0.reshape(Bs, CONV_W - 1, nch, LANES), pk['conv_w'].reshape(CONV_W, nch, LANES),
        sm.reshape(Bs, 1, LANES), z, pk['gdn_norm_g'], s0)
    x1 = _merge(x, o_fox.reshape(Bs, FOX_W), o_gdn.reshape(Bs, GDN_VW), P, pk, tm=Bs)
    y = _channel_and_ple(x1, p3.reshape(Bs, -1), pk, tm=Bs)
    return (y.reshape(Bs, 1, D), k.reshape(Bs, 1, FOX_HEADS, FOX_HD), v.reshape(Bs, 1, FOX_HEADS, FOX_HD),
            logf.reshape(Bs, 1, FOX_HEADS), s_new, conv_new.reshape(Bs, CONV_W - 1, CONV_CH))


def kernel(x_prompt, x_sample, p_prompt, p_sample, cache_fox_k, cache_fox_v, cache_fox_logf, state_gdn, state_conv, page_table, norm_mix_g, w_in, fox_f_bias, fox_q_norm, fox_k_norm, gdn_conv_w, gdn_a_log, gdn_dt_bias, gdn_norm_g, w_up_fox, w_up_gdn, w_out, norm_ffn_g, peer_w_q, peer_sub_keys, peer_u, peer_v, norm_ple_g, w_ple, w_ple_gate):
    depth = w_in.shape[0]
    D = x_prompt.shape[-1]
    xp, xs = x_prompt, x_sample
    outs = [[] for _ in range(10)]
    for i in range(depth):
        lw = {
            'norm_mix_g': norm_mix_g[i], 'w_in': w_in[i], 'fox_f_bias': fox_f_bias[i],
            'fox_q_norm': fox_q_norm[i], 'fox_k_norm': fox_k_norm[i], 'gdn_conv_w': gdn_conv_w[i],
            'gdn_a_log': gdn_a_log[i], 'gdn_dt_bias': gdn_dt_bias[i], 'gdn_norm_g': gdn_norm_g[i],
            'w_up_fox': w_up_fox[i], 'w_up_gdn': w_up_gdn[i], 'w_out': w_out[i], 'norm_ffn_g': norm_ffn_g[i],
            'peer_w_q': peer_w_q[i], 'peer_sub_keys': peer_sub_keys[i], 'peer_u': peer_u[i], 'peer_v': peer_v[i],
            'norm_ple_g': norm_ple_g[i], 'w_ple': w_ple[i], 'w_ple_gate': w_ple_gate[i],
        }
        pk = _pack_layer(lw, D)
        xp, kp, vp, lfp, sp, cp = _block_prompt(xp, p_prompt[i], pk)
        xs, ks_, vs_, lfs, ss, cs = _block_sample(xs, p_sample[i], cache_fox_k, cache_fox_v, cache_fox_logf,
                                                  state_gdn[i], state_conv[i], page_table, i, pk)
        for lst, val in zip(outs, (kp, vp, lfp, sp, cp, ks_, vs_, lfs, ss, cs)):
            lst.append(val)
    return (xp, xs) + tuple(jnp.stack(o) for o in outs)
```
